```python
import jax, jax.numpy as jnp
from jax import lax
import numpy as np

D_MODEL = 4096
BATCH = 2
SEQ = 4096
DEPTH = 2

N_EVEN = (DEPTH + 1) // 2
N_ODD = DEPTH // 2

MLA_HEADS = D_MODEL // 256
MLA_NOPE = 128
MLA_ROPE = 64
MLA_V = 128
Q_LORA = D_MODEL // 4
KV_LORA = 512
ROPE_THETA = 10000.0
Q_BLOCK = 128

CONV_DIM = D_MODEL - MLA_HEADS * MLA_V
CONV_K = 3

MIX_IN = Q_LORA + KV_LORA + MLA_ROPE + 3 * CONV_DIM

LRU_WIDTH = D_MODEL
LRU_HEADS = 16
LRU_HEAD_DIM = LRU_WIDTH // LRU_HEADS
LRU_CONV_K = 4
LRU_C = 8.0

D_FF = ((8 * D_MODEL // 3 + 255) // 256) * 256
N_EXPERTS = 8
TOP_K = 2
D_FF_EXPERT = D_MODEL

EPS = 1e-6

kernel_name = "hybrid_mla_shortconv_rglru_moe_adaln"


def rms_norm(x, g):
    x32 = x.astype(jnp.float32)
    y = x32 * lax.rsqrt(jnp.mean(x32 * x32, axis=-1, keepdims=True) + EPS)
    return y.astype(x.dtype) * g


def ada_modulation(c, w, b):
    mod = jax.nn.silu(c) @ w + b
    return jnp.split(mod[:, None, :], 3, axis=-1)


def rope_angles(positions):
    inv = 1.0 / (ROPE_THETA ** (jnp.arange(0, MLA_ROPE, 2, dtype=jnp.float32) / MLA_ROPE))
    ang = positions.astype(jnp.float32)[..., None] * inv
    return jnp.cos(ang), jnp.sin(ang)


def apply_rope(x, cos, sin):
    x1, x2 = jnp.split(x, 2, axis=-1)
    cos = cos.astype(x.dtype)
    sin = sin.astype(x.dtype)
    return jnp.concatenate([x1 * cos - x2 * sin, x1 * sin + x2 * cos], axis=-1)


def causal_dwconv(u, w):
    k = w.shape[0]
    s = u.shape[1]
    up = jnp.pad(u, ((0, 0), (k - 1, 0), (0, 0)))
    return sum(up[:, j:j + s] * w[j] for j in range(k))


def mla_attention(q_nope, q_rope, k_nope, k_rope, v):
    b, s, h, _ = q_nope.shape
    nb = s // Q_BLOCK
    scale = (MLA_NOPE + MLA_ROPE) ** -0.5
    key_pos = jnp.arange(s)

    def to_blocks(q):
        return q.reshape(b, nb, Q_BLOCK, h, q.shape[-1]).transpose(1, 0, 2, 3, 4)

    def one_block(args):
        qn, qr, i = args
        sc = (jnp.einsum('bqhd,bkhd->bhqk', qn, k_nope, preferred_element_type=jnp.float32)
              + jnp.einsum('bqhr,bkr->bhqk', qr, k_rope, preferred_element_type=jnp.float32))
        q_pos = i * Q_BLOCK + jnp.arange(Q_BLOCK)
        causal = key_pos[None, :] <= q_pos[:, None]
        p = jax.nn.softmax(jnp.where(causal, sc * scale, -jnp.inf), axis=-1)
        return jnp.einsum('bhqk,bkhd->bqhd', p.astype(v.dtype), v)

    out = lax.map(one_block, (to_blocks(q_nope), to_blocks(q_rope), jnp.arange(nb)))
    return out.transpose(1, 0, 2, 3, 4).reshape(b, s, h * MLA_V)


def attn_conv_mixer(h, cos, sin, w_in, q_norm_g, kv_norm_g, w_uq, w_ukv, conv_w, w_out):
    b, s, _ = h.shape
    proj = h @ w_in
    o1 = Q_LORA
    o2 = o1 + KV_LORA
    o3 = o2 + MLA_ROPE
    o4 = o3 + CONV_DIM
    o5 = o4 + CONV_DIM
    cq, ckv, k_rope, gate_b, gate_c, u = jnp.split(proj, [o1, o2, o3, o4, o5], axis=-1)
    q = (rms_norm(cq, q_norm_g) @ w_uq).reshape(b, s, MLA_HEADS, MLA_NOPE + MLA_ROPE)
    kv = (rms_norm(ckv, kv_norm_g) @ w_ukv).reshape(b, s, MLA_HEADS, MLA_NOPE + MLA_V)
    q_nope, q_rope = q[..., :MLA_NOPE], q[..., MLA_NOPE:]
    k_nope, v = kv[..., :MLA_NOPE], kv[..., MLA_NOPE:]
    q_rope = apply_rope(q_rope, cos[:, :, None, :], sin[:, :, None, :])
    k_rope = apply_rope(k_rope, cos, sin)
    attn = mla_attention(q_nope, q_rope, k_nope, k_rope, v)
    conv = gate_b * causal_dwconv(gate_c * u, conv_w)
    return jnp.concatenate([attn, conv], axis=-1) @ w_out


def _linear_recurrence_combine(left, right):
    a_l, b_l = left
    a_r, b_r = right
    return a_l * a_r, a_r * b_l + b_r


def recurrent_mixer(h, w_in, conv_w, conv_b, gate_a_w, gate_a_b, gate_x_w, gate_x_b, lam, w_out):
    b, s, _ = h.shape
    gate_br, xb = jnp.split(h @ w_in, 2, axis=-1)
    gate_br = jax.nn.gelu(gate_br, approximate=True)
    xb = causal_dwconv(xb, conv_w) + conv_b
    xh = xb.reshape(b, s, LRU_HEADS, LRU_HEAD_DIM)
    r = jax.nn.sigmoid(jnp.einsum('bshi,hij->bshj', xh, gate_a_w).reshape(b, s, LRU_WIDTH) + gate_a_b)
    i = jax.nn.sigmoid(jnp.einsum('bshi,hij->bshj', xh, gate_x_w).reshape(b, s, LRU_WIDTH) + gate_x_b)
    log_a = (LRU_C * r.astype(jnp.float32)) * jax.nn.log_sigmoid(lam.astype(jnp.float32))
    a = jnp.exp(log_a)
    mult = jnp.sqrt(-jnp.expm1(2.0 * log_a))
    bt = mult * (i * xb).astype(jnp.float32)
    _, hs = lax.associative_scan(_linear_recurrence_combine, (a, bt), axis=1)
    y = hs.astype(h.dtype) * gate_br
    return y @ w_out


def swiglu(h, w1, w3, w2):
    return (jax.nn.silu(h @ w1) * (h @ w3)) @ w2


def moe_swiglu(h, router_w, router_b, w1, w3, w2):
    b, s, d = h.shape
    t = h.reshape(b * s, d)
    logits = (t @ router_w).astype(jnp.float32) + router_b.astype(jnp.float32)
    top_v, top_i = lax.top_k(logits, TOP_K)
    top_w = jax.nn.softmax(top_v, axis=-1)
    combine = jnp.sum(jax.nn.one_hot(top_i, N_EXPERTS, dtype=jnp.float32) * top_w[..., None], axis=1)
    out = jnp.zeros_like(t)
    for e in range(N_EXPERTS):
        out = out + combine[:, e:e + 1].astype(t.dtype) * swiglu(t, w1[e], w3[e], w2[e])
    return out.reshape(b, s, d)


def setup_inputs(seed: int = 0) -> dict:
    key = jax.random.key(seed)
    ks = iter(jax.random.split(key, 48))
    f32 = jnp.float32

    def dense(shape, fan_in):
        return jax.random.normal(next(ks), shape, f32) * (fan_in ** -0.5)

    def gain(shape):
        return 1.0 + 0.02 * jax.random.normal(next(ks), shape, f32)

    def bias(shape, s=0.02):
        return s * jax.random.normal(next(ks), shape, f32)

    x = jax.random.normal(next(ks), (BATCH, SEQ, D_MODEL), f32)
    c = jax.random.normal(next(ks), (BATCH, D_MODEL), f32)
    start = jax.random.randint(next(ks), (BATCH, 1), 0, 4096, dtype=jnp.int32)
    positions = start + jnp.arange(SEQ, dtype=jnp.int32)[None, :]

    a_target = jax.random.uniform(next(ks), (N_ODD, LRU_WIDTH), f32, 0.9, 0.999)
    s_lam = a_target ** (1.0 / LRU_C)
    lam = jnp.log(s_lam) - jnp.log1p(-s_lam)

    return {
        "x": x,
        "c": c,
        "positions": positions,
        "ada_w": dense((2 * DEPTH, D_MODEL, 3 * D_MODEL), D_MODEL),
        "ada_b": bias((2 * DEPTH, 3 * D_MODEL)),
        "norm_g": gain((2 * DEPTH, D_MODEL)),
        "even_w_in": dense((N_EVEN, D_MODEL, MIX_IN), D_MODEL),
        "even_q_norm_g": gain((N_EVEN, Q_LORA)),
        "even_kv_norm_g": gain((N_EVEN, KV_LORA)),
        "even_w_uq": dense((N_EVEN, Q_LORA, MLA_HEADS * (MLA_NOPE + MLA_ROPE)), Q_LORA),
        "even_w_ukv": dense((N_EVEN, KV_LORA, MLA_HEADS * (MLA_NOPE + MLA_V)), KV_LORA),
        "even_conv_w": dense((N_EVEN, CONV_K, CONV_DIM), CONV_K),
        "even_w_out": dense((N_EVEN, MLA_HEADS * MLA_V + CONV_DIM, D_MODEL), D_MODEL),
        "even_ffn_w1": dense((N_EVEN, D_MODEL, D_FF), D_MODEL),
        "even_ffn_w3": dense((N_EVEN, D_MODEL, D_FF), D_MODEL),
        "even_ffn_w2": dense((N_EVEN, D_FF, D_MODEL), D_FF),
        "odd_w_in": dense((N_ODD, D_MODEL, 2 * LRU_WIDTH), D_MODEL),
        "odd_conv_w": dense((N_ODD, LRU_CONV_K, LRU_WIDTH), LRU_CONV_K),
        "odd_conv_b": bias((N_ODD, LRU_WIDTH)),
        "odd_gate_a_w": dense((N_ODD, LRU_HEADS, LRU_HEAD_DIM, LRU_HEAD_DIM), LRU_HEAD_DIM),
        "odd_gate_a_b": bias((N_ODD, LRU_WIDTH)),
        "odd_gate_x_w": dense((N_ODD, LRU_HEADS, LRU_HEAD_DIM, LRU_HEAD_DIM), LRU_HEAD_DIM),
        "odd_gate_x_b": bias((N_ODD, LRU_WIDTH)),
        "odd_lambda": lam,
        "odd_w_out": dense((N_ODD, LRU_WIDTH, D_MODEL), LRU_WIDTH),
        "odd_router_w": dense((N_ODD, D_MODEL, N_EXPERTS), D_MODEL),
        "odd_router_b": bias((N_ODD, N_EXPERTS), 0.01),
        "odd_exp_w1": dense((N_ODD, N_EXPERTS, D_MODEL, D_FF_EXPERT), D_MODEL),
        "odd_exp_w3": dense((N_ODD, N_EXPERTS, D_MODEL, D_FF_EXPERT), D_MODEL),
        "odd_exp_w2": dense((N_ODD, N_EXPERTS, D_FF_EXPERT, D_MODEL), D_FF_EXPERT),
        "final_norm_g": gain((D_MODEL,)),
    }


def reference(x, c, positions, ada_w, ada_b, norm_g,
              even_w_in, even_q_norm_g, even_kv_norm_g, even_w_uq, even_w_ukv, even_conv_w,
              even_w_out, even_ffn_w1, even_ffn_w3, even_ffn_w2,
              odd_w_in, odd_conv_w, odd_conv_b, odd_gate_a_w, odd_gate_a_b, odd_gate_x_w,
              odd_gate_x_b, odd_lambda, odd_w_out, odd_router_w, odd_router_b,
              odd_exp_w1, odd_exp_w3, odd_exp_w2, final_norm_g):
    cos, sin = rope_angles(positions)
    for layer in range(DEPTH):
        j = layer // 2
        shift, scale, gate = ada_modulation(c, ada_w[2 * layer], ada_b[2 * layer])
        h = rms_norm(x, norm_g[2 * layer]) * (1 + scale) + shift
        if layer % 2 == 0:
            y = attn_conv_mixer(h, cos, sin, even_w_in[j], even_q_norm_g[j], even_kv_norm_g[j],
                                even_w_uq[j], even_w_ukv[j], even_conv_w[j], even_w_out[j])
        else:
            y = recurrent_mixer(h, odd_w_in[j], odd_conv_w[j], odd_conv_b[j], odd_gate_a_w[j],
                                odd_gate_a_b[j], odd_gate_x_w[j], odd_gate_x_b[j],
                                odd_lambda[j], odd_w_out[j])
        x = x + gate * y
        shift, scale, gate = ada_modulation(c, ada_w[2 * layer + 1], ada_b[2 * layer + 1])
        h = rms_norm(x, norm_g[2 * layer + 1]) * (1 + scale) + shift
        if layer % 2 == 0:
            y = swiglu(h, even_ffn_w1[j], even_ffn_w3[j], even_ffn_w2[j])
        else:
            y = moe_swiglu(h, odd_router_w[j], odd_router_b[j], odd_exp_w1[j], odd_exp_w3[j], odd_exp_w2[j])
        x = x + gate * y
    return rms_norm(x, final_norm_g)
```

```python
import functools

import jax
import jax.numpy as jnp
from jax import lax
from jax.experimental import pallas as pl
from jax.experimental.pallas import tpu as pltpu

F32 = jnp.float32
BF16 = jnp.bfloat16

EPS = 1e-6
ROPE_THETA = 10000.0
LRU_C = 8.0
MLA_NOPE = 128
MLA_ROPE = 64
MLA_V = 128
QK_HEAD = MLA_NOPE + 2 * MLA_ROPE
LANE = 128
ADA_ROWS = 16
VMEM_LIMIT_BYTES = 56 * 1024 * 1024


def _params(*sem):
    return pltpu.CompilerParams(dimension_semantics=sem, vmem_limit_bytes=VMEM_LIMIT_BYTES)


def _tile(dim, pref, mult=LANE):
    if dim <= pref:
        return dim
    t = (pref // mult) * mult
    while t > mult and dim % t:
        t -= mult
    assert dim % t == 0, (dim, pref)
    return t


def _round_up(x, m):
    return (x + m - 1) // m * m


def _ada_kernel(c_ref, w_ref, b_ref, o_ref):
    c = c_ref[...]
    sc = (c * jax.nn.sigmoid(c)).astype(BF16)
    o_ref[...] = jnp.dot(sc, w_ref[...].astype(BF16), preferred_element_type=F32) + b_ref[...]


def _ada_modulation(c, ada_w, ada_b):
    nmod, d, d3 = ada_w.shape
    b = c.shape[0]
    tn = _tile(d3, 512)
    cp = jnp.zeros((ADA_ROWS, d), F32).at[:b].set(c)
    out = pl.pallas_call(
        _ada_kernel,
        out_shape=jax.ShapeDtypeStruct((nmod, ADA_ROWS, d3), F32),
        grid=(nmod, d3 // tn),
        in_specs=[
            pl.BlockSpec((ADA_ROWS, d), lambda i, j: (0, 0)),
            pl.BlockSpec((None, d, tn), lambda i, j: (i, 0, j)),
            pl.BlockSpec((None, 1, tn), lambda i, j: (i, 0, j)),
        ],
        out_specs=pl.BlockSpec((None, ADA_ROWS, tn), lambda i, j: (i, 0, j)),
        compiler_params=_params("parallel", "parallel"),
        name="ada_modulation",
    )(cp, ada_w, ada_b.reshape(nmod, 1, d3))
    return out[:, :b].reshape(nmod, b, 3, d)


def _mod_spec(li, d_blk, rows_per_batch_tiles, col_map=None):
    if col_map is None:
        return pl.BlockSpec((None, None, 3, d_blk),
                            lambda m, *_: (li, m // rows_per_batch_tiles, 0, 0))
    return pl.BlockSpec((None, None, 3, d_blk),
                        lambda m, n, *_: (li, m // rows_per_batch_tiles, 0, n))


NORM_CHUNK = 16


def _norm_rows(x, g, mod_ref):
    var = jnp.mean(x * x, axis=-1, keepdims=True)
    y = x * lax.rsqrt(var + EPS)
    return (y * g) * (1.0 + mod_ref[1:2, :]) + mod_ref[0:1, :]


def _norm_kernel(x_ref, g_ref, mod_ref, h_ref, *, tm):
    def body(r, carry):
        r0 = pl.multiple_of(r * NORM_CHUNK, NORM_CHUNK)
        x = x_ref[pl.ds(r0, NORM_CHUNK), :]
        h_ref[pl.ds(r0, NORM_CHUNK), :] = _norm_rows(x, g_ref[...], mod_ref).astype(h_ref.dtype)
        return carry
    lax.fori_loop(0, tm // NORM_CHUNK, body, 0)


def _norm_mod(x, g, mods, li, seq):
    t, d = x.shape
    tm = _tile(seq, 512, NORM_CHUNK)
    return pl.pallas_call(
        functools.partial(_norm_kernel, tm=tm),
        out_shape=jax.ShapeDtypeStruct((t, d), BF16),
        grid=(t // tm,),
        in_specs=[
            pl.BlockSpec((tm, d), lambda m: (m, 0)),
            pl.BlockSpec((1, d), lambda m: (0, 0)),
            _mod_spec(li, d, seq // tm),
        ],
        out_specs=pl.BlockSpec((tm, d), lambda m: (m, 0)),
        compiler_params=_params("parallel"),
        name=f"norm_mod_{li}",
    )(x, g.reshape(1, d), mods)


def _mm_kernel(*refs, n_lhs, n_w, n_extra, n_out, nk, epilogue):
    lhs = refs[:n_lhs]
    ws = refs[n_lhs:n_lhs + n_w]
    extra = refs[n_lhs + n_w:n_lhs + n_w + n_extra]
    outs = refs[n_lhs + n_w + n_extra:n_lhs + n_w + n_extra + n_out]
    accs = refs[n_lhs + n_w + n_extra + n_out:]

    def partial_product(w_ref):
        off, tot = 0, None
        for l_ref in lhs:
            kk = l_ref.shape[1]
            part = jnp.dot(l_ref[...], w_ref[off:off + kk, :], preferred_element_type=F32)
            tot = part if tot is None else tot + part
            off += kk
        return tot

    if nk == 1:
        epilogue([partial_product(w) for w in ws], extra, outs)
        return

    k = pl.program_id(2)

    @pl.when(k == 0)
    def _():
        for acc, w in zip(accs, ws):
            acc[...] = partial_product(w)

    @pl.when(k > 0)
    def _():
        for acc, w in zip(accs, ws):
            acc[...] += partial_product(w)

    @pl.when(k == nk - 1)
    def _():
        epilogue([acc[...] for acc in accs], extra, outs)


def _matmul(lhs_list, w_list, w_col_offsets, epilogue, extras, extra_specs, out_shapes,
            *, tm, tn, tk=None, name):
    t = lhs_list[0].shape[0]
    ktot = sum(l.shape[1] for l in lhs_list)
    n = out_shapes[0].shape[1]
    if tk is None:
        nk = 1
        lhs_specs = [pl.BlockSpec((tm, l.shape[1]), lambda m, j, k: (m, 0)) for l in lhs_list]
        w_rows = ktot
    else:
        assert len(lhs_list) == 1 and ktot % tk == 0
        nk = ktot // tk
        lhs_specs = [pl.BlockSpec((tm, tk), lambda m, j, k: (m, k))]
        w_rows = tk
    w_specs = [pl.BlockSpec((w_rows, tn), functools.partial(lambda m, j, k, off: (k, j + off), off=off))
               for off in w_col_offsets]
    kern = functools.partial(_mm_kernel, n_lhs=len(lhs_list), n_w=len(w_list), n_extra=len(extras),
                             n_out=len(out_shapes), nk=nk, epilogue=epilogue)
    scratch = [pltpu.VMEM((tm, tn), F32) for _ in w_list] if nk > 1 else []
    return pl.pallas_call(
        kern,
        out_shape=out_shapes,
        grid=(t // tm, n // tn, nk),
        in_specs=lhs_specs + w_specs + list(extra_specs),
        out_specs=[pl.BlockSpec((tm, tn), lambda m, j, k: (m, j)) for _ in out_shapes],
        scratch_shapes=scratch,
        compiler_params=_params("parallel", "parallel", "arbitrary"),
        name=name,
    )(*lhs_list, *w_list, *extras)


def _epi_store(accs, extra, outs):
    outs[0][...] = accs[0].astype(outs[0].dtype)


def _epi_residual(accs, extra, outs):
    x_ref, mod_ref = extra
    outs[0][...] = x_ref[...] + mod_ref[2:3, :] * accs[0]


def _epi_swiglu(accs, extra, outs):
    a, b = accs
    outs[0][...] = (jax.nn.silu(a) * b).astype(outs[0].dtype)


def _epi_gelu_pair(accs, extra, outs):
    a, b = accs
    outs[0][...] = jax.nn.gelu(a, approximate=True).astype(outs[0].dtype)
    outs[1][...] = b.astype(outs[1].dtype)


def _residual_matmul(lhs_list, w, x, mods, li, seq, *, tm, tn, tk=None, name):
    t, d = x.shape
    extras = [x, mods]
    extra_specs = [pl.BlockSpec((tm, tn), lambda m, j, k: (m, j)),
                   _mod_spec(li, tn, seq // tm, col_map=True)]
    return _matmul(lhs_list, [w], [0], _epi_residual, extras, extra_specs,
                   [jax.ShapeDtypeStruct((t, d), F32)], tm=tm, tn=tn, tk=tk, name=name)[0]


def _rope_kernel(pos_ref, inv_ref, o_ref):
    ang = pos_ref[...].astype(F32) * inv_ref[...]
    lane = lax.broadcasted_iota(jnp.int32, ang.shape, 1)
    o_ref[...] = jnp.where(lane < MLA_ROPE, jnp.cos(ang), jnp.sin(ang))


def _rope_table(positions):
    t = positions.size
    tm = _tile(t, 1024, 8)
    half = MLA_ROPE // 2
    inv = 1.0 / (ROPE_THETA ** (jnp.arange(0, MLA_ROPE, 2, dtype=F32) / MLA_ROPE))
    inv4 = jnp.tile(inv, 4).reshape(1, 4 * half)
    return pl.pallas_call(
        _rope_kernel,
        out_shape=jax.ShapeDtypeStruct((t, LANE), F32),
        grid=(t // tm,),
        in_specs=[pl.BlockSpec((tm, 1), lambda m: (m, 0)),
                  pl.BlockSpec((1, LANE), lambda m: (0, 0))],
        out_specs=pl.BlockSpec((tm, LANE), lambda m: (m, 0)),
        compiler_params=_params("parallel"),
        name="rope_table",
    )(positions.reshape(t, 1), inv4)


def _qproj_kernel(cq_ref, g_ref, rt_ref, w_ref, o_ref, nq_ref, *, heads_per_tile, scale):
    @pl.when(pl.program_id(1) == 0)
    def _():
        cq = cq_ref[...].astype(F32)
        var = jnp.mean(cq * cq, axis=-1, keepdims=True)
        nq_ref[...] = ((cq * lax.rsqrt(var + EPS)) * g_ref[...]).astype(BF16)

    res = jnp.dot(nq_ref[...], w_ref[...], preferred_element_type=F32)
    rt = rt_ref[...] * scale
    for hh in range(heads_per_tile):
        c0 = hh * QK_HEAD
        o_ref[:, c0:c0 + MLA_NOPE] = (res[:, c0:c0 + MLA_NOPE] * scale).astype(o_ref.dtype)
        o_ref[:, c0 + MLA_NOPE:c0 + QK_HEAD] = (res[:, c0 + MLA_NOPE:c0 + QK_HEAD] * rt).astype(o_ref.dtype)


def _q_projection(proj, g_q, rt, w_uq_p, heads, q_lora):
    t = proj.shape[0]
    tm = _tile(t, 512, 16)
    hpt = min(heads, 4)
    tn = hpt * QK_HEAD
    scale = (MLA_NOPE + MLA_ROPE) ** -0.5
    return pl.pallas_call(
        functools.partial(_qproj_kernel, heads_per_tile=hpt, scale=scale),
        out_shape=jax.ShapeDtypeStruct((t, heads * QK_HEAD), BF16),
        grid=(t // tm, heads // hpt),
        in_specs=[
            pl.BlockSpec((tm, q_lora), lambda m, j: (m, 0)),
            pl.BlockSpec((1, q_lora), lambda m, j: (0, 0)),
            pl.BlockSpec((tm, LANE), lambda m, j: (m, 0)),
            pl.BlockSpec((q_lora, tn), lambda m, j: (0, j)),
        ],
        out_specs=pl.BlockSpec((tm, tn), lambda m, j: (m, j)),
        scratch_shapes=[pltpu.VMEM((tm, q_lora), BF16)],
        compiler_params=_params("parallel", "arbitrary"),
        name="q_projection",
    )(proj, g_q.reshape(1, q_lora), rt, w_uq_p)


def _kvproj_kernel(ckv_ref, kr_ref, g_ref, rt_ref, w_ref, kn_ref, v_ref, kro_ref, *, hv):
    ckv = ckv_ref[...].astype(F32)
    var = jnp.mean(ckv * ckv, axis=-1, keepdims=True)
    nkv = ((ckv * lax.rsqrt(var + EPS)) * g_ref[...]).astype(BF16)
    res = jnp.dot(nkv, w_ref[...], preferred_element_type=F32)
    kn_ref[...] = res[:, :hv].astype(kn_ref.dtype)
    v_ref[...] = res[:, hv:].astype(v_ref.dtype)
    kv = kr_ref[...].astype(F32) * rt_ref[...]
    kro_ref[...] = (kv + pltpu.roll(kv, MLA_ROPE, 1)).astype(kro_ref.dtype)


def _kv_projection(proj, g_kv, rt, w_ukv_p, heads, q_lora, kv_lora):
    t = proj.shape[0]
    tm = _tile(t, 512, 16)
    hv = heads * MLA_NOPE
    return pl.pallas_call(
        functools.partial(_kvproj_kernel, hv=hv),
        out_shape=[jax.ShapeDtypeStruct((t, hv), BF16), jax.ShapeDtypeStruct((t, hv), BF16),
                   jax.ShapeDtypeStruct((t, LANE), BF16)],
        grid=(t // tm,),
        in_specs=[
            pl.BlockSpec((tm, kv_lora), lambda m: (m, q_lora // kv_lora)),
            pl.BlockSpec((tm, LANE), lambda m: (m, (q_lora + kv_lora) // LANE)),
            pl.BlockSpec((1, kv_lora), lambda m: (0, 0)),
            pl.BlockSpec((tm, LANE), lambda m: (m, 0)),
            pl.BlockSpec((kv_lora, 2 * hv), lambda m: (0, 0)),
        ],
        out_specs=[pl.BlockSpec((tm, hv), lambda m: (m, 0)), pl.BlockSpec((tm, hv), lambda m: (m, 0)),
                   pl.BlockSpec((tm, LANE), lambda m: (m, 0))],
        compiler_params=_params("parallel"),
        name="kv_projection",
    )(proj, proj, g_kv.reshape(1, kv_lora), rt, w_ukv_p)


def _attn_kernel(q_ref, kn_ref, kr_ref, v_ref, o_ref, *, tq):
    qi = pl.program_id(2)
    q = q_ref[...]

    def block(ki, carry, masked):
        m, l, acc = carry
        ks = pl.multiple_of(ki * tq, tq)
        k = jnp.concatenate([kn_ref[pl.ds(ks, tq), :], kr_ref[pl.ds(ks, tq), :]], axis=1)
        s = lax.dot_general(q, k, (((1,), (1,)), ((), ())), preferred_element_type=F32)
        if masked:
            row = lax.broadcasted_iota(jnp.int32, s.shape, 0)
            col = lax.broadcasted_iota(jnp.int32, s.shape, 1)
            s = jnp.where(col <= row, s, -jnp.inf)
        m_new = jnp.maximum(m, jnp.max(s, axis=1, keepdims=True))
        alpha = jnp.exp(m - m_new)
        p = jnp.exp(s - m_new)
        l = alpha * l + jnp.sum(p, axis=1, keepdims=True)
        acc = alpha * acc + jnp.dot(p.astype(BF16), v_ref[pl.ds(ks, tq), :], preferred_element_type=F32)
        return m_new, l, acc

    init = (jnp.full((tq, 1), -jnp.inf, F32), jnp.zeros((tq, 1), F32), jnp.zeros((tq, MLA_V), F32))
    carry = lax.fori_loop(0, qi, lambda ki, c: block(ki, c, False), init)
    _, l, acc = block(qi, carry, True)
    o_ref[...] = (acc / l).astype(o_ref.dtype)


def _attention(q, kn, kr, v, batch, seq, heads):
    t = q.shape[0]
    tq = _tile(seq, 512, 16)
    nq = seq // tq
    return pl.pallas_call(
        functools.partial(_attn_kernel, tq=tq),
        out_shape=jax.ShapeDtypeStruct((t, heads * MLA_V), BF16),
        grid=(batch, heads, nq),
        in_specs=[
            pl.BlockSpec((tq, QK_HEAD), lambda b, h, i: (b * nq + i, h)),
            pl.BlockSpec((seq, MLA_NOPE), lambda b, h, i: (b, h)),
            pl.BlockSpec((seq, LANE), lambda b, h, i: (b, 0)),
            pl.BlockSpec((seq, MLA_V), lambda b, h, i: (b, h)),
        ],
        out_specs=pl.BlockSpec((tq, MLA_V), lambda b, h, i: (b * nq + i, h)),
        compiler_params=_params("parallel", "parallel", "arbitrary"),
        name="mla_attention",
    )(q, kn, kr, v)


HALO = 16


def _conv3_kernel(gb_ref, gc_ref, u_ref, gch_ref, uh_ref, w_ref, o_ref, ext_ref, *, tm, tiles_per_seq, taps):
    first = (pl.program_id(0) % tiles_per_seq) == 0
    halo = gch_ref[...].astype(F32) * uh_ref[...].astype(F32)
    ext_ref[0:HALO, :] = jnp.where(first, 0.0, halo)
    ext_ref[HALO:, :] = gc_ref[...].astype(F32) * u_ref[...].astype(F32)
    y = None
    for j in range(taps):
        off = HALO - (taps - 1 - j)
        term = w_ref[j:j + 1, :] * ext_ref[off:off + tm, :]
        y = term if y is None else y + term
    o_ref[...] = (gb_ref[...].astype(F32) * y).astype(o_ref.dtype)


def _gated_conv(proj, conv_w, seq, seg0, conv_dim):
    t = proj.shape[0]
    taps = conv_w.shape[0]
    tm = _tile(seq, 512, HALO)
    tc = _tile(conv_dim, 512)
    assert seg0 % tc == 0
    ob, oc, ou = seg0 // tc, (seg0 + conv_dim) // tc, (seg0 + 2 * conv_dim) // tc
    hb = tm // HALO

    def halo_map(off):
        return lambda m, c: (jnp.maximum(m * hb - 1, 0), off + c)

    return pl.pallas_call(
        functools.partial(_conv3_kernel, tm=tm, tiles_per_seq=seq // tm, taps=taps),
        out_shape=jax.ShapeDtypeStruct((t, conv_dim), BF16),
        grid=(t // tm, conv_dim // tc),
        in_specs=[
            pl.BlockSpec((tm, tc), lambda m, c: (m, ob + c)),
            pl.BlockSpec((tm, tc), lambda m, c: (m, oc + c)),
            pl.BlockSpec((tm, tc), lambda m, c: (m, ou + c)),
            pl.BlockSpec((HALO, tc), halo_map(oc)),
            pl.BlockSpec((HALO, tc), halo_map(ou)),
            pl.BlockSpec((taps, tc), lambda m, c: (0, c)),
        ],
        out_specs=pl.BlockSpec((tm, tc), lambda m, c: (m, c)),
        scratch_shapes=[pltpu.VMEM((tm + HALO, tc), F32)],
        compiler_params=_params("parallel", "parallel"),
        name="gated_conv3",
    )(proj, proj, proj, proj, proj, conv_w)


GROUP = 8


def _lru_kernel(xb_ref, gbr_ref, cw_ref, cb_ref, wa_ref, ba_ref, wx_ref, bx_ref, lam_ref, o_ref,
                ext_ref, a_ref, b_ref, hc_ref, *, tm, hd, hp, taps):
    @pl.when(pl.program_id(2) == 0)
    def _():
        ext_ref[0:GROUP, :] = jnp.zeros((GROUP, ext_ref.shape[1]), F32)
        hc_ref[...] = jnp.zeros(hc_ref.shape, F32)

    ext_ref[GROUP:, :] = xb_ref[...].astype(F32)
    xb = cb_ref[...]
    for j in range(taps):
        off = GROUP - (taps - 1 - j)
        xb = xb + cw_ref[j:j + 1, :] * ext_ref[off:off + tm, :]
    ext_ref[0:GROUP, :] = ext_ref[tm:tm + GROUP, :]

    xb16 = xb.astype(BF16)
    rs, is_ = [], []
    for h in range(hp):
        xh = xb16[:, h * hd:(h + 1) * hd]
        rs.append(jnp.dot(xh, wa_ref[h], preferred_element_type=F32))
        is_.append(jnp.dot(xh, wx_ref[h], preferred_element_type=F32))
    r = jax.nn.sigmoid(jnp.concatenate(rs, axis=1) + ba_ref[...])
    ig = jax.nn.sigmoid(jnp.concatenate(is_, axis=1) + bx_ref[...])
    lam = lam_ref[...]
    log_sig = -(jnp.maximum(-lam, 0.0) + jnp.log(1.0 + jnp.exp(-jnp.abs(lam))))
    log_a = (LRU_C * r) * log_sig
    a = jnp.exp(log_a)
    th = jnp.tanh(log_a)
    mult = jnp.sqrt(-2.0 * th / (1.0 - th))
    a_ref[...] = a
    b_ref[...] = mult * (ig * xb)

    rowg = lax.broadcasted_iota(jnp.int32, (GROUP, a_ref.shape[1]), 0)

    def group(gi, hc):
        r0 = pl.multiple_of(gi * GROUP, GROUP)
        ag = a_ref[pl.ds(r0, GROUP), :]
        bg = b_ref[pl.ds(r0, GROUP), :]
        for dist in (1, 2, 4):
            keep = rowg >= dist
            ap = jnp.where(keep, pltpu.roll(ag, dist, 0), 1.0)
            bp = jnp.where(keep, pltpu.roll(bg, dist, 0), 0.0)
            bg = ag * bp + bg
            ag = ag * ap
        hs = ag * hc + bg
        y = hs * gbr_ref[pl.ds(r0, GROUP), :].astype(F32)
        b_ref[pl.ds(r0, GROUP), :] = y
        return jnp.broadcast_to(hs[GROUP - 1:GROUP, :], hs.shape)

    hc_ref[...] = lax.fori_loop(0, tm // GROUP, group, hc_ref[...])
    o_ref[...] = b_ref[...].astype(o_ref.dtype)


def _rglru(xb_pre, gate_br, conv_w, conv_b, wa, ba, wx, bx, lam, batch, seq):
    t, width = xb_pre.shape
    heads, hd, _ = wa.shape
    taps = conv_w.shape[0]
    hp = 2 if heads % 2 == 0 else 1
    c = hp * hd
    tm = _tile(seq, 256, 16)
    nt = seq // tm
    row = lambda b, h, i: (b * nt + i, h)
    vec = lambda b, h, i: (0, h)
    return pl.pallas_call(
        functools.partial(_lru_kernel, tm=tm, hd=hd, hp=hp, taps=taps),
        out_shape=jax.ShapeDtypeStruct((t, width), BF16),
        grid=(batch, heads // hp, nt),
        in_specs=[
            pl.BlockSpec((tm, c), row),
            pl.BlockSpec((tm, c), row),
            pl.BlockSpec((taps, c), vec),
            pl.BlockSpec((1, c), vec),
            pl.BlockSpec((hp, hd, hd), lambda b, h, i: (h, 0, 0)),
            pl.BlockSpec((1, c), vec),
            pl.BlockSpec((hp, hd, hd), lambda b, h, i: (h, 0, 0)),
            pl.BlockSpec((1, c), vec),
            pl.BlockSpec((1, c), vec),
        ],
        out_specs=pl.BlockSpec((tm, c), row),
        scratch_shapes=[pltpu.VMEM((tm + GROUP, c), F32), pltpu.VMEM((tm, c), F32),
                        pltpu.VMEM((tm, c), F32), pltpu.VMEM((GROUP, c), F32)],
        compiler_params=_params("parallel", "parallel", "arbitrary"),
        name="rglru",
    )(xb_pre, gate_br, conv_w, conv_b.reshape(1, width), wa, ba.reshape(1, width), wx,
      bx.reshape(1, width), lam.reshape(1, width))


def _router_kernel(x_ref, g_ref, mod_ref, rw_ref, rb_ref, h_ref, meta_ref, cnt_ref, run_ref, *, tm, n_exp):
    @pl.when(pl.program_id(0) == 0)
    def _():
        run_ref[...] = jnp.zeros(run_ref.shape, F32)

    def body(r, carry):
        r0 = pl.multiple_of(r * GROUP, GROUP)
        h_ref[pl.ds(r0, GROUP), :] = _norm_rows(x_ref[pl.ds(r0, GROUP), :], g_ref[...], mod_ref)
        return carry
    lax.fori_loop(0, tm // GROUP, body, 0)

    logits = jnp.dot(h_ref[...], rw_ref[...], preferred_element_type=F32,
                     precision=lax.Precision.HIGHEST) + rb_ref[...]
    lane = lax.broadcasted_iota(jnp.int32, logits.shape, 1)
    lg = jnp.where(lane < n_exp, logits, -jnp.inf)
    m1 = jnp.max(lg, axis=1, keepdims=True)
    i1 = jnp.min(jnp.where(lg == m1, lane, LANE), axis=1, keepdims=True)
    lg2 = jnp.where(lane == i1, -jnp.inf, lg)
    m2 = jnp.max(lg2, axis=1, keepdims=True)
    i2 = jnp.min(jnp.where(lg2 == m2, lane, LANE), axis=1, keepdims=True)
    e2 = jnp.exp(m2 - m1)
    den = 1.0 + e2
    w1 = 1.0 / den
    w2 = e2 / den

    hit1 = lane == i1
    hit2 = lane == i2
    sel = jnp.where(hit1 | hit2, 1.0, 0.0)
    rowi = lax.broadcasted_iota(jnp.int32, (tm, tm), 0)
    coli = lax.broadcasted_iota(jnp.int32, (tm, tm), 1)
    earlier = jnp.where(coli < rowi, 1.0, 0.0).astype(BF16)
    rank = jnp.dot(earlier, sel.astype(BF16), preferred_element_type=F32) + run_ref[0:1, :]
    r1 = jnp.sum(jnp.where(hit1, rank, 0.0), axis=1, keepdims=True)
    r2 = jnp.sum(jnp.where(hit2, rank, 0.0), axis=1, keepdims=True)
    run_ref[...] = run_ref[...] + jnp.sum(sel, axis=0, keepdims=True)
    cnt_ref[...] = run_ref[...]

    meta = jnp.where(lane == 0, i1.astype(F32), 0.0)
    meta = jnp.where(lane == 1, i2.astype(F32), meta)
    meta = jnp.where(lane == 2, r1, meta)
    meta = jnp.where(lane == 3, r2, meta)
    meta = jnp.where(lane == 4, w1, meta)
    meta = jnp.where(lane == 5, w2, meta)
    meta_ref[...] = meta


def _router(x, g, mods, li, seq, router_w, router_b):
    t, d = x.shape
    n_exp = router_w.shape[1]
    tm = _tile(seq, 256, 16)
    rw = jnp.zeros((d, LANE), F32).at[:, :n_exp].set(router_w)
    rb = jnp.zeros((1, LANE), F32).at[0, :n_exp].set(router_b)
    return pl.pallas_call(
        functools.partial(_router_kernel, tm=tm, n_exp=n_exp),
        out_shape=[jax.ShapeDtypeStruct((t, d), F32), jax.ShapeDtypeStruct((t, LANE), F32),
                   jax.ShapeDtypeStruct((GROUP, LANE), F32)],
        grid=(t // tm,),
        in_specs=[
            pl.BlockSpec((tm, d), lambda m: (m, 0)),
            pl.BlockSpec((1, d), lambda m: (0, 0)),
            _mod_spec(li, d, seq // tm),
            pl.BlockSpec((d, LANE), lambda m: (0, 0)),
            pl.BlockSpec((1, LANE), lambda m: (0, 0)),
        ],
        out_specs=[pl.BlockSpec((tm, d), lambda m: (m, 0)), pl.BlockSpec((tm, LANE), lambda m: (m, 0)),
                   pl.BlockSpec((GROUP, LANE), lambda m: (0, 0))],
        scratch_shapes=[pltpu.VMEM((GROUP, LANE), F32)],
        compiler_params=_params("arbitrary"),
        name="moe_router",
    )(x, g.reshape(1, d), mods, rw, rb)


DISPATCH_WINDOW = 32


def _dispatch_kernel(pos_ref, h_hbm, xs_hbm, zbuf, zsem, sem, *, n_tok, n_zero, zr):
    zbuf[...] = jnp.zeros(zbuf.shape, zbuf.dtype)

    def zero_copy(c):
        return pltpu.make_async_copy(zbuf, xs_hbm.at[pl.ds(c * zr, zr), :], zsem)

    def zstart(c, carry):
        zero_copy(c).start()
        return carry
    lax.fori_loop(0, n_zero, zstart, 0)

    def zwait(c, carry):
        zero_copy(c).wait()
        return carry
    lax.fori_loop(0, n_zero, zwait, 0)

    def row_copy(tok, slot):
        return pltpu.make_async_copy(h_hbm.at[pl.ds(tok, 1), :], xs_hbm.at[pl.ds(slot, 1), :], sem)

    def wait_token():
        row_copy(0, 0).wait()
        row_copy(0, 0).wait()

    def body(tok, carry):
        row_copy(tok, pos_ref[2 * tok]).start()
        row_copy(tok, pos_ref[2 * tok + 1]).start()

        @pl.when(tok >= DISPATCH_WINDOW)
        def _():
            wait_token()
        return carry
    lax.fori_loop(0, n_tok, body, 0)

    def drain(i, carry):
        wait_token()
        return carry
    lax.fori_loop(0, min(DISPATCH_WINDOW, n_tok), drain, 0)


def _dispatch(h, pos, n_rows, zr):
    t, d = h.shape
    return pl.pallas_call(
        functools.partial(_dispatch_kernel, n_tok=t, n_zero=n_rows // zr, zr=zr),
        out_shape=jax.ShapeDtypeStruct((n_rows, d), h.dtype),
        grid_spec=pltpu.PrefetchScalarGridSpec(
            num_scalar_prefetch=1,
            grid=(1,),
            in_specs=[pl.BlockSpec(memory_space=pl.ANY)],
            out_specs=pl.BlockSpec(memory_space=pl.ANY),
            scratch_shapes=[pltpu.VMEM((zr, d), h.dtype), pltpu.SemaphoreType.DMA(()),
                            pltpu.SemaphoreType.DMA(())],
        ),
        compiler_params=_params("arbitrary"),
        name="moe_dispatch",
    )(pos, h)


def _gm1_kernel(te_ref, tv_ref, x_ref, w1_ref, w3_ref, o_ref, xb_ref):
    i = pl.program_id(0)
    valid = tv_ref[i] > 0

    @pl.when(valid)
    def _():
        @pl.when(pl.program_id(1) == 0)
        def _():
            xb_ref[...] = x_ref[...].astype(BF16)
        a = jnp.dot(xb_ref[...], w1_ref[...], preferred_element_type=F32)
        b = jnp.dot(xb_ref[...], w3_ref[...], preferred_element_type=F32)
        o_ref[...] = (jax.nn.silu(a) * b).astype(o_ref.dtype)

    @pl.when(jnp.logical_not(valid))
    def _():
        o_ref[...] = jnp.zeros(o_ref.shape, o_ref.dtype)


def _gm2_kernel(te_ref, tv_ref, a_ref, w_ref, o_ref):
    i = pl.program_id(0)
    valid = tv_ref[i] > 0

    @pl.when(valid)
    def _():
        o_ref[...] = jnp.dot(a_ref[...], w_ref[...], preferred_element_type=F32).astype(o_ref.dtype)

    @pl.when(jnp.logical_not(valid))
    def _():
        o_ref[...] = jnp.zeros(o_ref.shape, o_ref.dtype)


def _expert_ffn(xs, te, tv, w1, w3, w2, tm_e):
    r, d = xs.shape
    _, _, f = w1.shape
    n_tiles = r // tm_e
    tf = _tile(f, 512)
    nf = f // tf

    def wmap(i, j, te_ref, tv_ref):
        return (te_ref[i], 0, jnp.where(tv_ref[i] > 0, j, nf - 1))

    a_s = pl.pallas_call(
        _gm1_kernel,
        out_shape=jax.ShapeDtypeStruct((r, f), BF16),
        grid_spec=pltpu.PrefetchScalarGridSpec(
            num_scalar_prefetch=2,
            grid=(n_tiles, nf),
            in_specs=[pl.BlockSpec((tm_e, d), lambda i, j, te_ref, tv_ref: (i, 0)),
                      pl.BlockSpec((None, d, tf), wmap),
                      pl.BlockSpec((None, d, tf), wmap)],
            out_specs=pl.BlockSpec((tm_e, tf), lambda i, j, te_ref, tv_ref: (i, j)),
            scratch_shapes=[pltpu.VMEM((tm_e, d), BF16)],
        ),
        compiler_params=_params("parallel", "arbitrary"),
        name="moe_expert_up",
    )(te, tv, xs, w1, w3)

    tn = _tile(d, 1024)
    nn = d // tn

    def w2map(i, j, te_ref, tv_ref):
        return (te_ref[i], 0, jnp.where(tv_ref[i] > 0, j, nn - 1))

    return pl.pallas_call(
        _gm2_kernel,
        out_shape=jax.ShapeDtypeStruct((r, d), F32),
        grid_spec=pltpu.PrefetchScalarGridSpec(
            num_scalar_prefetch=2,
            grid=(n_tiles, nn),
            in_specs=[pl.BlockSpec((tm_e, f), lambda i, j, te_ref, tv_ref: (i, 0)),
                      pl.BlockSpec((None, f, tn), w2map)],
            out_specs=pl.BlockSpec((tm_e, tn), lambda i, j, te_ref, tv_ref: (i, j)),
        ),
        compiler_params=_params("parallel", "arbitrary"),
        name="moe_expert_down",
    )(te, tv, a_s, w2)


def _combine_kernel(pos_ref, ys_hbm, x_ref, meta_ref, mod_ref, g_ref, o_ref, buf, sem, *, tm):
    base = pl.program_id(0) * tm

    def row_copy(r, k, slot):
        return pltpu.make_async_copy(ys_hbm.at[pl.ds(slot, 1), :], buf.at[k, pl.ds(r, 1), :], sem.at[k])

    def issue(r, carry):
        for k in range(2):
            row_copy(r, k, pos_ref[2 * (base + r) + k]).start()
        return carry
    lax.fori_loop(0, tm, issue, 0)

    def wait(r, carry):
        for k in range(2):
            row_copy(r, k, 0).wait()
        return carry
    lax.fori_loop(0, tm, wait, 0)

    y = meta_ref[:, 4:5] * buf[0] + meta_ref[:, 5:6] * buf[1]
    xn = x_ref[...] + mod_ref[2:3, :] * y
    var = jnp.mean(xn * xn, axis=-1, keepdims=True)
    o_ref[...] = (xn * lax.rsqrt(var + EPS)) * g_ref[...]


def _combine_final(ys, pos, x, meta, mods, li, seq, g_final):
    t, d = x.shape
    tm = _tile(seq, 128, 8)
    return pl.pallas_call(
        functools.partial(_combine_kernel, tm=tm),
        out_shape=jax.ShapeDtypeStruct((t, d), F32),
        grid_spec=pltpu.PrefetchScalarGridSpec(
            num_scalar_prefetch=1,
            grid=(t // tm,),
            in_specs=[pl.BlockSpec(memory_space=pl.ANY),
                      pl.BlockSpec((tm, d), lambda m, pos_ref: (m, 0)),
                      pl.BlockSpec((tm, LANE), lambda m, pos_ref: (m, 0)),
                      pl.BlockSpec((None, None, 3, d), lambda m, pos_ref: (li, m // (seq // tm), 0, 0)),
                      pl.BlockSpec((1, d), lambda m, pos_ref: (0, 0))],
            out_specs=pl.BlockSpec((tm, d), lambda m, pos_ref: (m, 0)),
            scratch_shapes=[pltpu.VMEM((2, tm, d), F32), pltpu.SemaphoreType.DMA((2,))],
        ),
        compiler_params=_params("arbitrary"),
        name="moe_combine_final_norm",
    )(pos, ys, x, meta, mods, g_final.reshape(1, d))


def _rot_half_cols(w):
    half = w.shape[-1] // 2
    return jnp.concatenate([-w[..., half:], w[..., :half]], axis=-1)


def _prep_w_in(w_in, q_lora, kv_lora, conv_dim, seg0):
    d = w_in.shape[0]
    o1 = q_lora + kv_lora
    o2 = o1 + MLA_ROPE
    k_rope = w_in[:, o1:o2]
    head = jnp.concatenate([w_in[:, :o1], k_rope, _rot_half_cols(k_rope)], axis=1)
    pad = jnp.zeros((d, seg0 - head.shape[1]), w_in.dtype)
    return jnp.concatenate([head, pad, w_in[:, o2:]], axis=1).astype(BF16)


def _prep_w_uq(w_uq, heads):
    ql = w_uq.shape[0]
    w = w_uq.reshape(ql, heads, MLA_NOPE + MLA_ROPE)
    rope = w[..., MLA_NOPE:]
    return jnp.concatenate([w[..., :MLA_NOPE], rope, _rot_half_cols(rope)], axis=-1
                           ).reshape(ql, heads * QK_HEAD).astype(BF16)


def _prep_w_ukv(w_ukv, heads):
    kvl = w_ukv.shape[0]
    w = w_ukv.reshape(kvl, heads, MLA_NOPE + MLA_V)
    return jnp.concatenate([w[..., :MLA_NOPE].reshape(kvl, heads * MLA_NOPE),
                            w[..., MLA_NOPE:].reshape(kvl, heads * MLA_V)], axis=1).astype(BF16)


def kernel(x, c, positions, ada_w, ada_b, norm_g, even_w_in, even_q_norm_g, even_kv_norm_g, even_w_uq, even_w_ukv, even_conv_w, even_w_out, even_ffn_w1, even_ffn_w3, even_ffn_w2, odd_w_in, odd_conv_w, odd_conv_b, odd_gate_a_w, odd_gate_a_b, odd_gate_x_w, odd_gate_x_b, odd_lambda, odd_w_out, odd_router_w, odd_router_b, odd_exp_w1, odd_exp_w3, odd_exp_w2, final_norm_g):
    batch, seq, d = x.shape
    t = batch * seq
    q_lora = even_q_norm_g.shape[1]
    kv_lora = even_kv_norm_g.shape[1]
    heads = even_w_uq.shape[2] // (MLA_NOPE + MLA_ROPE)
    conv_dim = even_conv_w.shape[2]
    d_ff = even_ffn_w1.shape[2]
    n_exp = odd_router_w.shape[2]
    assert even_w_in.shape[0] == 1 and odd_w_in.shape[0] == 1, "one layer of each type"

    xf = x.reshape(t, d)
    mods = _ada_modulation(c, ada_w, ada_b)
    rt = _rope_table(positions)
    tm = _tile(seq, 1024, 16)

    tcv = _tile(conv_dim, 512)
    seg0 = _round_up(q_lora + kv_lora + LANE, tcv)
    w_in0 = _prep_w_in(even_w_in[0], q_lora, kv_lora, conv_dim, seg0)
    n_in0 = w_in0.shape[1]
    h = _norm_mod(xf, norm_g[0], mods, 0, seq)
    proj = _matmul([h], [w_in0], [0], _epi_store, [], [], [jax.ShapeDtypeStruct((t, n_in0), BF16)],
                   tm=tm, tn=_tile(n_in0, 1024), name="in_proj_0")[0]
    q = _q_projection(proj, even_q_norm_g[0], rt, _prep_w_uq(even_w_uq[0], heads), heads, q_lora)
    kn, v, kr = _kv_projection(proj, even_kv_norm_g[0], rt, _prep_w_ukv(even_w_ukv[0], heads),
                               heads, q_lora, kv_lora)
    attn = _attention(q, kn, kr, v, batch, seq, heads)
    conv = _gated_conv(proj, even_conv_w[0], seq, seg0, conv_dim)
    xf = _residual_matmul([attn, conv], even_w_out[0].astype(BF16), xf, mods, 0, seq,
                          tm=tm, tn=_tile(d, 512), name="out_proj_0")

    f_pad = _round_up(d_ff, 1024)
    w1 = jnp.pad(even_ffn_w1[0], ((0, 0), (0, f_pad - d_ff))).astype(BF16)
    w3 = jnp.pad(even_ffn_w3[0], ((0, 0), (0, f_pad - d_ff))).astype(BF16)
    w2 = jnp.pad(even_ffn_w2[0], ((0, f_pad - d_ff), (0, 0))).astype(BF16)
    h = _norm_mod(xf, norm_g[1], mods, 1, seq)
    act = _matmul([h], [w1, w3], [0, 0], _epi_swiglu, [], [], [jax.ShapeDtypeStruct((t, f_pad), BF16)],
                  tm=tm, tn=_tile(f_pad, 512), name="ffn_up")[0]
    xf = _residual_matmul([act], w2, xf, mods, 1, seq, tm=tm, tn=_tile(d, 1024),
                          tk=_tile(f_pad, 3072), name="ffn_down")

    width = odd_conv_w.shape[2]
    w_in1 = odd_w_in[0].astype(BF16)
    tn1 = _tile(width, 512)
    h = _norm_mod(xf, norm_g[2], mods, 2, seq)
    gate_br, xb_pre = _matmul([h], [w_in1, w_in1], [0, width // tn1], _epi_gelu_pair, [], [],
                              [jax.ShapeDtypeStruct((t, width), BF16)] * 2, tm=tm, tn=tn1,
                              name="in_proj_1")
    y = _rglru(xb_pre, gate_br, odd_conv_w[0], odd_conv_b[0], odd_gate_a_w[0].astype(BF16),
               odd_gate_a_b[0], odd_gate_x_w[0].astype(BF16), odd_gate_x_b[0], odd_lambda[0], batch, seq)
    xf = _residual_matmul([y], odd_w_out[0].astype(BF16), xf, mods, 2, seq, tm=tm, tn=_tile(d, 512),
                          name="out_proj_1")

    tm_e = _tile(seq, 512, 16)
    n_rows = 2 * t + n_exp * tm_e
    n_tiles = n_rows // tm_e
    h32, meta, cnt = _router(xf, norm_g[3], mods, 3, seq, odd_router_w[0], odd_router_b[0])
    counts = cnt[0, :n_exp].astype(jnp.int32)
    tiles_e = (counts + tm_e - 1) // tm_e
    ends = jnp.cumsum(tiles_e)
    start_rows = (ends - tiles_e) * tm_e
    e_idx = meta[:, 0:2].astype(jnp.int32)
    pos = (jnp.take(start_rows, e_idx) + meta[:, 2:4].astype(jnp.int32)).reshape(2 * t)
    tid = jnp.arange(n_tiles, dtype=jnp.int32)
    tv = (tid < ends[-1]).astype(jnp.int32)
    te = jnp.minimum(jnp.sum((tid[:, None] >= ends[None, :]).astype(jnp.int32), axis=1), n_exp - 1)
    te = jnp.where(tv > 0, te, te[jnp.maximum(ends[-1] - 1, 0)])
    xs = _dispatch(h32, pos, n_rows, tm_e)
    ys = _expert_ffn(xs, te, tv, odd_exp_w1[0].astype(BF16), odd_exp_w3[0].astype(BF16),
                     odd_exp_w2[0].astype(BF16), tm_e)
    out = _combine_final(ys, pos, xf, meta, mods, 3, seq, final_norm_g)
    return out.reshape(batch, seq, d)
```

```python
import functools

import jax
import jax.numpy as jnp
from jax import lax
from jax.experimental import pallas as pl
from jax.experimental.pallas import tpu as pltpu

F32 = jnp.float32
BF16 = jnp.bfloat16

EPS = 1e-6
ROPE_THETA = 10000.0
LRU_C = 8.0
MLA_NOPE = 128
MLA_ROPE = 64
MLA_V = 128
QK_HEAD = MLA_NOPE + 2 * MLA_ROPE
LANE = 128
ADA_ROWS = 16
VMEM_LIMIT_BYTES = 56 * 1024 * 1024


def _params(grid_rank):
    return pltpu.CompilerParams(dimension_semantics=("arbitrary",) * grid_rank,
                                vmem_limit_bytes=VMEM_LIMIT_BYTES)


def _tile(dim, pref, mult=LANE):
    if dim <= pref:
        return dim
    t = (pref // mult) * mult
    while t > mult and dim % t:
        t -= mult
    assert dim % t == 0, (dim, pref)
    return t


def _round_up(x, m):
    return (x + m - 1) // m * m


def _ada_kernel(c_ref, w_ref, b_ref, o_ref):
    c = c_ref[...]
    sc = (c * jax.nn.sigmoid(c)).astype(BF16)
    o_ref[...] = jnp.dot(sc, w_ref[...].astype(BF16), preferred_element_type=F32) + b_ref[...]


def _ada_modulation(c, ada_w, ada_b):
    nmod, d, d3 = ada_w.shape
    b = c.shape[0]
    tn = _tile(d3, 512)
    cp = jnp.zeros((ADA_ROWS, d), F32).at[:b].set(c)
    out = pl.pallas_call(
        _ada_kernel,
        out_shape=jax.ShapeDtypeStruct((nmod, ADA_ROWS, d3), F32),
        grid=(nmod, d3 // tn),
        in_specs=[
            pl.BlockSpec((ADA_ROWS, d), lambda i, j: (0, 0)),
            pl.BlockSpec((None, d, tn), lambda i, j: (i, 0, j)),
            pl.BlockSpec((None, 1, tn), lambda i, j: (i, 0, j)),
        ],
        out_specs=pl.BlockSpec((None, ADA_ROWS, tn), lambda i, j: (i, 0, j)),
        compiler_params=_params(2),
        name="ada_modulation",
    )(cp, ada_w, ada_b.reshape(nmod, 1, d3))
    return out[:, :b].reshape(nmod, b, 3, d)


def _mod_spec(li, d_blk, rows_per_batch_tiles, col_map=None):
    if col_map is None:
        return pl.BlockSpec((None, None, 3, d_blk),
                            lambda m, *_: (li, m // rows_per_batch_tiles, 0, 0))
    return pl.BlockSpec((None, None, 3, d_blk),
                        lambda m, n, *_: (li, m // rows_per_batch_tiles, 0, n))


NORM_CHUNK = 16


def _norm_rows(x, g, mod_ref):
    var = jnp.mean(x * x, axis=-1, keepdims=True)
    y = x * lax.rsqrt(var + EPS)
    return (y * g) * (1.0 + mod_ref[1:2, :]) + mod_ref[0:1, :]


def _norm_kernel(x_ref, g_ref, mod_ref, h_ref, *, tm):
    def body(r, carry):
        r0 = pl.multiple_of(r * NORM_CHUNK, NORM_CHUNK)
        x = x_ref[pl.ds(r0, NORM_CHUNK), :]
        h_ref[pl.ds(r0, NORM_CHUNK), :] = _norm_rows(x, g_ref[...], mod_ref).astype(h_ref.dtype)
        return carry
    lax.fori_loop(0, tm // NORM_CHUNK, body, 0)


def _norm_mod(x, g, mods, li, seq):
    t, d = x.shape
    tm = _tile(seq, 512, NORM_CHUNK)
    return pl.pallas_call(
        functools.partial(_norm_kernel, tm=tm),
        out_shape=jax.ShapeDtypeStruct((t, d), BF16),
        grid=(t // tm,),
        in_specs=[
            pl.BlockSpec((tm, d), lambda m: (m, 0)),
            pl.BlockSpec((1, d), lambda m: (0, 0)),
            _mod_spec(li, d, seq // tm),
        ],
        out_specs=pl.BlockSpec((tm, d), lambda m: (m, 0)),
        compiler_params=_params(1),
        name=f"norm_mod_{li}",
    )(x, g.reshape(1, d), mods)


def _mm_kernel(*refs, n_lhs, n_w, n_extra, n_out, nk, epilogue):
    lhs = refs[:n_lhs]
    ws = refs[n_lhs:n_lhs + n_w]
    extra = refs[n_lhs + n_w:n_lhs + n_w + n_extra]
    outs = refs[n_lhs + n_w + n_extra:n_lhs + n_w + n_extra + n_out]
    accs = refs[n_lhs + n_w + n_extra + n_out:]

    def partial_product(w_ref):
        off, tot = 0, None
        for l_ref in lhs:
            kk = l_ref.shape[1]
            part = jnp.dot(l_ref[...], w_ref[off:off + kk, :], preferred_element_type=F32)
            tot = part if tot is None else tot + part
            off += kk
        return tot

    if nk == 1:
        epilogue([partial_product(w) for w in ws], extra, outs)
        return

    k = pl.program_id(2)

    @pl.when(k == 0)
    def _():
        for acc, w in zip(accs, ws):
            acc[...] = partial_product(w)

    @pl.when(k > 0)
    def _():
        for acc, w in zip(accs, ws):
            acc[...] += partial_product(w)

    @pl.when(k == nk - 1)
    def _():
        epilogue([acc[...] for acc in accs], extra, outs)


def _matmul(lhs_list, w_list, w_col_offsets, epilogue, extras, extra_specs, out_shapes,
            *, tm, tn, tk=None, name):
    t = lhs_list[0].shape[0]
    ktot = sum(l.shape[1] for l in lhs_list)
    n = out_shapes[0].shape[1]
    if tk is None:
        nk = 1
        lhs_specs = [pl.BlockSpec((tm, l.shape[1]), lambda m, j, k: (m, 0)) for l in lhs_list]
        w_rows = ktot
    else:
        assert len(lhs_list) == 1 and ktot % tk == 0
        nk = ktot // tk
        lhs_specs = [pl.BlockSpec((tm, tk), lambda m, j, k: (m, k))]
        w_rows = tk
    w_specs = [pl.BlockSpec((w_rows, tn), functools.partial(lambda m, j, k, off: (k, j + off), off=off))
               for off in w_col_offsets]
    kern = functools.partial(_mm_kernel, n_lhs=len(lhs_list), n_w=len(w_list), n_extra=len(extras),
                             n_out=len(out_shapes), nk=nk, epilogue=epilogue)
    scratch = [pltpu.VMEM((tm, tn), F32) for _ in w_list] if nk > 1 else []
    return pl.pallas_call(
        kern,
        out_shape=out_shapes,
        grid=(t // tm, n // tn, nk),
        in_specs=lhs_specs + w_specs + list(extra_specs),
        out_specs=[pl.BlockSpec((tm, tn), lambda m, j, k: (m, j)) for _ in out_shapes],
        scratch_shapes=scratch,
        compiler_params=_params(3),
        name=name,
    )(*lhs_list, *w_list, *extras)


def _epi_store(accs, extra, outs):
    outs[0][...] = accs[0].astype(outs[0].dtype)


def _epi_residual(accs, extra, outs):
    x_ref, mod_ref = extra
    outs[0][...] = x_ref[...] + mod_ref[2:3, :] * accs[0]


def _epi_swiglu(accs, extra, outs):
    a, b = accs
    outs[0][...] = (jax.nn.silu(a) * b).astype(outs[0].dtype)


def _epi_gelu_pair(accs, extra, outs):
    a, b = accs
    outs[0][...] = jax.nn.gelu(a, approximate=True).astype(outs[0].dtype)
    outs[1][...] = b.astype(outs[1].dtype)


def _residual_matmul(lhs_list, w, x, mods, li, seq, *, tm, tn, tk=None, name):
    t, d = x.shape
    extras = [x, mods]
    extra_specs = [pl.BlockSpec((tm, tn), lambda m, j, k: (m, j)),
                   _mod_spec(li, tn, seq // tm, col_map=True)]
    return _matmul(lhs_list, [w], [0], _epi_residual, extras, extra_specs,
                   [jax.ShapeDtypeStruct((t, d), F32)], tm=tm, tn=tn, tk=tk, name=name)[0]


def _rope_kernel(pos_ref, inv_ref, o_ref):
    ang = pos_ref[...].astype(F32) * inv_ref[...]
    lane = lax.broadcasted_iota(jnp.int32, ang.shape, 1)
    o_ref[...] = jnp.where(lane < MLA_ROPE, jnp.cos(ang), jnp.sin(ang))


def _rope_table(positions):
    t = positions.size
    tm = _tile(t, 1024, 8)
    half = MLA_ROPE // 2
    inv = 1.0 / (ROPE_THETA ** (jnp.arange(0, MLA_ROPE, 2, dtype=F32) / MLA_ROPE))
    inv4 = jnp.tile(inv, 4).reshape(1, 4 * half)
    return pl.pallas_call(
        _rope_kernel,
        out_shape=jax.ShapeDtypeStruct((t, LANE), F32),
        grid=(t // tm,),
        in_specs=[pl.BlockSpec((tm, 1), lambda m: (m, 0)),
                  pl.BlockSpec((1, LANE), lambda m: (0, 0))],
        out_specs=pl.BlockSpec((tm, LANE), lambda m: (m, 0)),
        compiler_params=_params(1),
        name="rope_table",
    )(positions.reshape(t, 1), inv4)


def _qproj_kernel(cq_ref, g_ref, rt_ref, w_ref, o_ref, nq_ref, *, heads_per_tile, scale):
    @pl.when(pl.program_id(1) == 0)
    def _():
        cq = cq_ref[...].astype(F32)
        var = jnp.mean(cq * cq, axis=-1, keepdims=True)
        nq_ref[...] = ((cq * lax.rsqrt(var + EPS)) * g_ref[...]).astype(BF16)

    res = jnp.dot(nq_ref[...], w_ref[...], preferred_element_type=F32)
    rt = rt_ref[...] * scale
    for hh in range(heads_per_tile):
        c0 = hh * QK_HEAD
        o_ref[:, c0:c0 + MLA_NOPE] = (res[:, c0:c0 + MLA_NOPE] * scale).astype(o_ref.dtype)
        o_ref[:, c0 + MLA_NOPE:c0 + QK_HEAD] = (res[:, c0 + MLA_NOPE:c0 + QK_HEAD] * rt).astype(o_ref.dtype)


def _q_projection(proj, g_q, rt, w_uq_p, heads, q_lora):
    t = proj.shape[0]
    tm = _tile(t, 512, 16)
    hpt = min(heads, 4)
    tn = hpt * QK_HEAD
    scale = (MLA_NOPE + MLA_ROPE) ** -0.5
    return pl.pallas_call(
        functools.partial(_qproj_kernel, heads_per_tile=hpt, scale=scale),
        out_shape=jax.ShapeDtypeStruct((t, heads * QK_HEAD), BF16),
        grid=(t // tm, heads // hpt),
        in_specs=[
            pl.BlockSpec((tm, q_lora), lambda m, j: (m, 0)),
            pl.BlockSpec((1, q_lora), lambda m, j: (0, 0)),
            pl.BlockSpec((tm, LANE), lambda m, j: (m, 0)),
            pl.BlockSpec((q_lora, tn), lambda m, j: (0, j)),
        ],
        out_specs=pl.BlockSpec((tm, tn), lambda m, j: (m, j)),
        scratch_shapes=[pltpu.VMEM((tm, q_lora), BF16)],
        compiler_params=_params(2),
        name="q_projection",
    )(proj, g_q.reshape(1, q_lora), rt, w_uq_p)


def _kvproj_kernel(ckv_ref, kr_ref, g_ref, rt_ref, w_ref, kn_ref, v_ref, kro_ref, *, hv):
    ckv = ckv_ref[...].astype(F32)
    var = jnp.mean(ckv * ckv, axis=-1, keepdims=True)
    nkv = ((ckv * lax.rsqrt(var + EPS)) * g_ref[...]).astype(BF16)
    res = jnp.dot(nkv, w_ref[...], preferred_element_type=F32)
    kn_ref[...] = res[:, :hv].astype(kn_ref.dtype)
    v_ref[...] = res[:, hv:].astype(v_ref.dtype)
    kv = kr_ref[...].astype(F32) * rt_ref[...]
    kro_ref[...] = (kv + pltpu.roll(kv, MLA_ROPE, 1)).astype(kro_ref.dtype)


def _kv_projection(proj, g_kv, rt, w_ukv_p, heads, q_lora, kv_lora):
    t = proj.shape[0]
    tm = _tile(t, 512, 16)
    hv = heads * MLA_NOPE
    return pl.pallas_call(
        functools.partial(_kvproj_kernel, hv=hv),
        out_shape=[jax.ShapeDtypeStruct((t, hv), BF16), jax.ShapeDtypeStruct((t, hv), BF16),
                   jax.ShapeDtypeStruct((t, LANE), BF16)],
        grid=(t // tm,),
        in_specs=[
            pl.BlockSpec((tm, kv_lora), lambda m: (m, q_lora // kv_lora)),
            pl.BlockSpec((tm, LANE), lambda m: (m, (q_lora + kv_lora) // LANE)),
            pl.BlockSpec((1, kv_lora), lambda m: (0, 0)),
            pl.BlockSpec((tm, LANE), lambda m: (m, 0)),
            pl.BlockSpec((kv_lora, 2 * hv), lambda m: (0, 0)),
        ],
        out_specs=[pl.BlockSpec((tm, hv), lambda m: (m, 0)), pl.BlockSpec((tm, hv), lambda m: (m, 0)),
                   pl.BlockSpec((tm, LANE), lambda m: (m, 0))],
        compiler_params=_params(1),
        name="kv_projection",
    )(proj, proj, g_kv.reshape(1, kv_lora), rt, w_ukv_p)


def _attn_kernel(q_ref, kn_ref, kr_ref, v_ref, o_ref, *, tq, hpb):
    qi = pl.program_id(2)
    ones = jnp.ones((tq, MLA_V), BF16)

    def block(ki, carry, masked):
        ks = pl.multiple_of(ki * tq, tq)
        kr = kr_ref[pl.ds(ks, tq), :]
        new = []
        for hh in range(hpb):
            m, acc = carry[hh]
            q = q_ref[:, hh * QK_HEAD:(hh + 1) * QK_HEAD]
            k = jnp.concatenate([kn_ref[pl.ds(ks, tq), hh * MLA_NOPE:(hh + 1) * MLA_NOPE], kr], axis=1)
            s = lax.dot_general(q, k, (((1,), (1,)), ((), ())), preferred_element_type=F32)
            if masked:
                row = lax.broadcasted_iota(jnp.int32, s.shape, 0)
                col = lax.broadcasted_iota(jnp.int32, s.shape, 1)
                s = jnp.where(col <= row, s, -jnp.inf)
            m_new = jnp.maximum(m, jnp.max(s, axis=1, keepdims=True))
            alpha = jnp.exp(m - m_new)
            p = jnp.exp(s - m_new).astype(BF16)
            v1 = jnp.concatenate([v_ref[pl.ds(ks, tq), hh * MLA_V:(hh + 1) * MLA_V], ones], axis=1)
            acc = alpha * acc + jnp.dot(p, v1, preferred_element_type=F32)
            new.append((m_new, acc))
        return tuple(new)

    init = tuple((jnp.full((tq, 1), -jnp.inf, F32), jnp.zeros((tq, 2 * MLA_V), F32)) for _ in range(hpb))
    carry = lax.fori_loop(0, qi, lambda ki, c: block(ki, c, False), init)
    carry = block(qi, carry, True)
    for hh in range(hpb):
        acc = carry[hh][1]
        o_ref[:, hh * MLA_V:(hh + 1) * MLA_V] = (acc[:, :MLA_V] / acc[:, MLA_V:]).astype(o_ref.dtype)


def _attention(q, kn, kr, v, batch, seq, heads):
    t = q.shape[0]
    tq = _tile(seq, 512, 16)
    nq = seq // tq
    hpb = 2 if heads % 2 == 0 else 1
    return pl.pallas_call(
        functools.partial(_attn_kernel, tq=tq, hpb=hpb),
        out_shape=jax.ShapeDtypeStruct((t, heads * MLA_V), BF16),
        grid=(batch, heads // hpb, nq),
        in_specs=[
            pl.BlockSpec((tq, hpb * QK_HEAD), lambda b, h, i: (b * nq + i, h)),
            pl.BlockSpec((seq, hpb * MLA_NOPE), lambda b, h, i: (b, h)),
            pl.BlockSpec((seq, LANE), lambda b, h, i: (b, 0)),
            pl.BlockSpec((seq, hpb * MLA_V), lambda b, h, i: (b, h)),
        ],
        out_specs=pl.BlockSpec((tq, hpb * MLA_V), lambda b, h, i: (b * nq + i, h)),
        compiler_params=_params(3),
        name="mla_attention",
    )(q, kn, kr, v)


HALO = 16


def _conv3_kernel(gb_ref, gc_ref, u_ref, gch_ref, uh_ref, w_ref, o_ref, ext_ref, *, tm, tiles_per_seq, taps):
    first = (pl.program_id(0) % tiles_per_seq) == 0
    halo = gch_ref[...].astype(F32) * uh_ref[...].astype(F32)
    ext_ref[0:HALO, :] = jnp.where(first, 0.0, halo)
    ext_ref[HALO:, :] = gc_ref[...].astype(F32) * u_ref[...].astype(F32)
    y = None
    for j in range(taps):
        off = HALO - (taps - 1 - j)
        term = w_ref[j:j + 1, :] * ext_ref[off:off + tm, :]
        y = term if y is None else y + term
    o_ref[...] = (gb_ref[...].astype(F32) * y).astype(o_ref.dtype)


def _gated_conv(proj, conv_w, seq, seg0, conv_dim):
    t = proj.shape[0]
    taps = conv_w.shape[0]
    tm = _tile(seq, 512, HALO)
    tc = _tile(conv_dim, 512)
    assert seg0 % tc == 0
    ob, oc, ou = seg0 // tc, (seg0 + conv_dim) // tc, (seg0 + 2 * conv_dim) // tc
    hb = tm // HALO

    def halo_map(off):
        return lambda m, c: (jnp.maximum(m * hb - 1, 0), off + c)

    return pl.pallas_call(
        functools.partial(_conv3_kernel, tm=tm, tiles_per_seq=seq // tm, taps=taps),
        out_shape=jax.ShapeDtypeStruct((t, conv_dim), BF16),
        grid=(t // tm, conv_dim // tc),
        in_specs=[
            pl.BlockSpec((tm, tc), lambda m, c: (m, ob + c)),
            pl.BlockSpec((tm, tc), lambda m, c: (m, oc + c)),
            pl.BlockSpec((tm, tc), lambda m, c: (m, ou + c)),
            pl.BlockSpec((HALO, tc), halo_map(oc)),
            pl.BlockSpec((HALO, tc), halo_map(ou)),
            pl.BlockSpec((taps, tc), lambda m, c: (0, c)),
        ],
        out_specs=pl.BlockSpec((tm, tc), lambda m, c: (m, c)),
        scratch_shapes=[pltpu.VMEM((tm + HALO, tc), F32)],
        compiler_params=_params(2),
        name="gated_conv3",
    )(proj, proj, proj, proj, proj, conv_w)


GROUP = 8


def _lru_kernel(xb_ref, gbr_ref, cw_ref, cb_ref, wa_ref, ba_ref, wx_ref, bx_ref, lam_ref, o_ref,
                ext_ref, a_ref, b_ref, hc_ref, *, tm, hd, hp, taps):
    @pl.when(pl.program_id(2) == 0)
    def _():
        ext_ref[0:GROUP, :] = jnp.zeros((GROUP, ext_ref.shape[1]), F32)
        hc_ref[...] = jnp.zeros(hc_ref.shape, F32)

    ext_ref[GROUP:, :] = xb_ref[...].astype(F32)
    xb = cb_ref[...]
    for j in range(taps):
        off = GROUP - (taps - 1 - j)
        xb = xb + cw_ref[j:j + 1, :] * ext_ref[off:off + tm, :]
    ext_ref[0:GROUP, :] = ext_ref[tm:tm + GROUP, :]

    xb16 = xb.astype(BF16)
    rs, is_ = [], []
    for h in range(hp):
        xh = xb16[:, h * hd:(h + 1) * hd]
        rs.append(jnp.dot(xh, wa_ref[h], preferred_element_type=F32))
        is_.append(jnp.dot(xh, wx_ref[h], preferred_element_type=F32))
    r = jax.nn.sigmoid(jnp.concatenate(rs, axis=1) + ba_ref[...])
    ig = jax.nn.sigmoid(jnp.concatenate(is_, axis=1) + bx_ref[...])
    lam = lam_ref[...]
    log_sig = -(jnp.maximum(-lam, 0.0) + jnp.log(1.0 + jnp.exp(-jnp.abs(lam))))
    log_a = (LRU_C * r) * log_sig
    a = jnp.exp(log_a)
    th = jnp.tanh(log_a)
    mult = jnp.sqrt(-2.0 * th / (1.0 - th))
    a_ref[...] = a
    b_ref[...] = mult * (ig * xb)

    rowg = lax.broadcasted_iota(jnp.int32, (GROUP, a_ref.shape[1]), 0)

    def group(gi, hc):
        r0 = pl.multiple_of(gi * GROUP, GROUP)
        ag = a_ref[pl.ds(r0, GROUP), :]
        bg = b_ref[pl.ds(r0, GROUP), :]
        for dist in (1, 2, 4):
            keep = rowg >= dist
            ap = jnp.where(keep, pltpu.roll(ag, dist, 0), 1.0)
            bp = jnp.where(keep, pltpu.roll(bg, dist, 0), 0.0)
            bg = ag * bp + bg
            ag = ag * ap
        hs = ag * hc + bg
        y = hs * gbr_ref[pl.ds(r0, GROUP), :].astype(F32)
        b_ref[pl.ds(r0, GROUP), :] = y
        return jnp.broadcast_to(hs[GROUP - 1:GROUP, :], hs.shape)

    hc_ref[...] = lax.fori_loop(0, tm // GROUP, group, hc_ref[...])
    o_ref[...] = b_ref[...].astype(o_ref.dtype)


def _rglru(xb_pre, gate_br, conv_w, conv_b, wa, ba, wx, bx, lam, batch, seq):
    t, width = xb_pre.shape
    heads, hd, _ = wa.shape
    taps = conv_w.shape[0]
    hp = 2 if heads % 2 == 0 else 1
    c = hp * hd
    tm = _tile(seq, 256, 16)
    nt = seq // tm
    row = lambda b, h, i: (b * nt + i, h)
    vec = lambda b, h, i: (0, h)
    return pl.pallas_call(
        functools.partial(_lru_kernel, tm=tm, hd=hd, hp=hp, taps=taps),
        out_shape=jax.ShapeDtypeStruct((t, width), BF16),
        grid=(batch, heads // hp, nt),
        in_specs=[
            pl.BlockSpec((tm, c), row),
            pl.BlockSpec((tm, c), row),
            pl.BlockSpec((taps, c), vec),
            pl.BlockSpec((1, c), vec),
            pl.BlockSpec((hp, hd, hd), lambda b, h, i: (h, 0, 0)),
            pl.BlockSpec((1, c), vec),
            pl.BlockSpec((hp, hd, hd), lambda b, h, i: (h, 0, 0)),
            pl.BlockSpec((1, c), vec),
            pl.BlockSpec((1, c), vec),
        ],
        out_specs=pl.BlockSpec((tm, c), row),
        scratch_shapes=[pltpu.VMEM((tm + GROUP, c), F32), pltpu.VMEM((tm, c), F32),
                        pltpu.VMEM((tm, c), F32), pltpu.VMEM((GROUP, c), F32)],
        compiler_params=_params(3),
        name="rglru",
    )(xb_pre, gate_br, conv_w, conv_b.reshape(1, width), wa, ba.reshape(1, width), wx,
      bx.reshape(1, width), lam.reshape(1, width))


def _router_kernel(x_ref, g_ref, mod_ref, rw_ref, rb_ref, h_ref, meta_ref, cnt_ref, run_ref, *, tm, n_exp):
    @pl.when(pl.program_id(0) == 0)
    def _():
        run_ref[...] = jnp.zeros(run_ref.shape, F32)

    def body(r, carry):
        r0 = pl.multiple_of(r * GROUP, GROUP)
        h_ref[pl.ds(r0, GROUP), :] = _norm_rows(x_ref[pl.ds(r0, GROUP), :], g_ref[...], mod_ref)
        return carry
    lax.fori_loop(0, tm // GROUP, body, 0)

    logits = jnp.dot(h_ref[...], rw_ref[...], preferred_element_type=F32,
                     precision=lax.Precision.HIGHEST) + rb_ref[...]
    lane = lax.broadcasted_iota(jnp.int32, logits.shape, 1)
    lg = jnp.where(lane < n_exp, logits, -jnp.inf)
    m1 = jnp.max(lg, axis=1, keepdims=True)
    i1 = jnp.min(jnp.where(lg == m1, lane, LANE), axis=1, keepdims=True)
    lg2 = jnp.where(lane == i1, -jnp.inf, lg)
    m2 = jnp.max(lg2, axis=1, keepdims=True)
    i2 = jnp.min(jnp.where(lg2 == m2, lane, LANE), axis=1, keepdims=True)
    e2 = jnp.exp(m2 - m1)
    den = 1.0 + e2
    w1 = 1.0 / den
    w2 = e2 / den

    hit1 = lane == i1
    hit2 = lane == i2
    sel = jnp.where(hit1 | hit2, 1.0, 0.0)
    rowi = lax.broadcasted_iota(jnp.int32, (tm, tm), 0)
    coli = lax.broadcasted_iota(jnp.int32, (tm, tm), 1)
    earlier = jnp.where(coli < rowi, 1.0, 0.0).astype(BF16)
    rank = jnp.dot(earlier, sel.astype(BF16), preferred_element_type=F32) + run_ref[0:1, :]
    r1 = jnp.sum(jnp.where(hit1, rank, 0.0), axis=1, keepdims=True)
    r2 = jnp.sum(jnp.where(hit2, rank, 0.0), axis=1, keepdims=True)
    run_ref[...] = run_ref[...] + jnp.sum(sel, axis=0, keepdims=True)
    cnt_ref[...] = run_ref[...]

    meta = jnp.where(lane == 0, i1.astype(F32), 0.0)
    meta = jnp.where(lane == 1, i2.astype(F32), meta)
    meta = jnp.where(lane == 2, r1, meta)
    meta = jnp.where(lane == 3, r2, meta)
    meta = jnp.where(lane == 4, w1, meta)
    meta = jnp.where(lane == 5, w2, meta)
    meta_ref[...] = meta


def _router(x, g, mods, li, seq, router_w, router_b):
    t, d = x.shape
    n_exp = router_w.shape[1]
    tm = _tile(seq, 256, 16)
    rw = jnp.zeros((d, LANE), F32).at[:, :n_exp].set(router_w)
    rb = jnp.zeros((1, LANE), F32).at[0, :n_exp].set(router_b)
    return pl.pallas_call(
        functools.partial(_router_kernel, tm=tm, n_exp=n_exp),
        out_shape=[jax.ShapeDtypeStruct((t, d), F32), jax.ShapeDtypeStruct((t, LANE), F32),
                   jax.ShapeDtypeStruct((GROUP, LANE), F32)],
        grid=(t // tm,),
        in_specs=[
            pl.BlockSpec((tm, d), lambda m: (m, 0)),
            pl.BlockSpec((1, d), lambda m: (0, 0)),
            _mod_spec(li, d, seq // tm),
            pl.BlockSpec((d, LANE), lambda m: (0, 0)),
            pl.BlockSpec((1, LANE), lambda m: (0, 0)),
        ],
        out_specs=[pl.BlockSpec((tm, d), lambda m: (m, 0)), pl.BlockSpec((tm, LANE), lambda m: (m, 0)),
                   pl.BlockSpec((GROUP, LANE), lambda m: (0, 0))],
        scratch_shapes=[pltpu.VMEM((GROUP, LANE), F32)],
        compiler_params=_params(1),
        name="moe_router",
    )(x, g.reshape(1, d), mods, rw, rb)


def _dispatch_kernel(pos_ref, zf_ref, h_ref, xs_hbm, zbuf, zsem, sem, *, tm, n_tiles, zr):
    @pl.when(pl.program_id(0) == 0)
    def _():
        zbuf[...] = jnp.zeros(zbuf.shape, zbuf.dtype)

        def zero_copy(c):
            return pltpu.make_async_copy(zbuf, xs_hbm.at[pl.ds(c * zr, zr), :], zsem)

        def zstart(c, carry):
            @pl.when(zf_ref[c] > 0)
            def _():
                zero_copy(c).start()
            return carry
        lax.fori_loop(0, n_tiles, zstart, 0)

        def zwait(c, carry):
            @pl.when(zf_ref[c] > 0)
            def _():
                zero_copy(c).wait()
            return carry
        lax.fori_loop(0, n_tiles, zwait, 0)

    base = pl.program_id(0) * tm

    def row_copy(r, slot):
        return pltpu.make_async_copy(h_ref.at[pl.ds(r, 1), :], xs_hbm.at[pl.ds(slot, 1), :], sem)

    def issue(r, carry):
        row_copy(r, pos_ref[2 * (base + r)]).start()
        row_copy(r, pos_ref[2 * (base + r) + 1]).start()
        return carry
    lax.fori_loop(0, tm, issue, 0)

    def wait(r, carry):
        row_copy(r, 0).wait()
        row_copy(r, 0).wait()
        return carry
    lax.fori_loop(0, tm, wait, 0)


def _dispatch(h, pos, zero_flag, n_rows, zr, seq):
    t, d = h.shape
    tm = _tile(seq, 256, 8)
    n_tiles = n_rows // zr
    return pl.pallas_call(
        functools.partial(_dispatch_kernel, tm=tm, n_tiles=n_tiles, zr=zr),
        out_shape=jax.ShapeDtypeStruct((n_rows, d), h.dtype),
        grid_spec=pltpu.PrefetchScalarGridSpec(
            num_scalar_prefetch=2,
            grid=(t // tm,),
            in_specs=[pl.BlockSpec((tm, d), lambda m, pos_ref, zf_ref: (m, 0))],
            out_specs=pl.BlockSpec(memory_space=pl.ANY),
            scratch_shapes=[pltpu.VMEM((zr, d), h.dtype), pltpu.SemaphoreType.DMA(()),
                            pltpu.SemaphoreType.DMA(())],
        ),
        compiler_params=_params(1),
        name="moe_dispatch",
    )(pos, zero_flag, h)


ITEM_ZERO, ITEM_COMPUTE, ITEM_FIRST, ITEM_FIRST_MORE = 0, 1, 2, 3


def _work_items(ends, tiles_e, n_tiles, ncol):
    n_exp = ends.shape[0]
    s = jnp.arange(ncol * n_tiles, dtype=jnp.int32)
    n_valid = ncol * ends[-1]
    e = jnp.minimum(jnp.sum((s[:, None] >= ncol * ends[None, :]).astype(jnp.int32), axis=1), n_exp - 1)
    te = jnp.maximum(jnp.take(tiles_e, e), 1)
    local = s - ncol * jnp.take(ends - tiles_e, e)
    col = local // te
    tile = jnp.take(ends - tiles_e, e) + local % te
    valid = s < n_valid
    first = valid & (local % te == 0)
    nxt = jnp.minimum(s + te, ncol * n_tiles - 1)
    more = (s + te) < n_valid
    rem = s - n_valid
    tile = jnp.where(valid, tile, ends[-1] + rem // ncol)
    col = jnp.where(valid, col, rem % ncol)
    kind = jnp.where(valid, jnp.where(first, jnp.where(more, ITEM_FIRST_MORE, ITEM_FIRST), ITEM_COMPUTE),
                     ITEM_ZERO)
    return tile, col, e, kind.astype(jnp.int32), jnp.take(e, nxt), jnp.take(col, nxt)


def _stage_weights(w_hbm_list, stage, wb, sem, kind, s, e_ref, c_ref, nxe_ref, nxc_ref, tn):
    def wcopy(e, c, k):
        cols = pl.ds(pl.multiple_of(c * tn, tn), tn)
        return pltpu.make_async_copy(w_hbm_list[k].at[e, :, cols], stage.at[k], sem.at[k])

    @pl.when(s == 0)
    def _():
        for k in range(len(w_hbm_list)):
            wcopy(e_ref[0], c_ref[0], k).start()

    @pl.when(kind >= ITEM_FIRST)
    def _():
        for k in range(len(w_hbm_list)):
            wcopy(e_ref[s], c_ref[s], k).wait()
            wb[k] = stage[k].astype(BF16)

        @pl.when(kind == ITEM_FIRST_MORE)
        def _():
            for k in range(len(w_hbm_list)):
                wcopy(nxe_ref[s], nxc_ref[s], k).start()


def _gm1_kernel(tile_ref, c_ref, e_ref, kind_ref, nxe_ref, nxc_ref, x_ref, w1_hbm, w3_hbm, o_ref,
                stage, wb, sem, *, tf):
    s = pl.program_id(0)
    kind = kind_ref[s]
    _stage_weights((w1_hbm, w3_hbm), stage, wb, sem, kind, s, e_ref, c_ref, nxe_ref, nxc_ref, tf)

    @pl.when(kind >= ITEM_COMPUTE)
    def _():
        x = x_ref[...].astype(BF16)
        a = jnp.dot(x, wb[0], preferred_element_type=F32)
        b = jnp.dot(x, wb[1], preferred_element_type=F32)
        o_ref[...] = (jax.nn.silu(a) * b).astype(o_ref.dtype)

    @pl.when(kind == ITEM_ZERO)
    def _():
        o_ref[...] = jnp.zeros(o_ref.shape, o_ref.dtype)


def _gm2_kernel(tile_ref, c_ref, e_ref, kind_ref, nxe_ref, nxc_ref, a_ref, w_hbm, o_ref,
                stage, wb, sem, *, tn):
    s = pl.program_id(0)
    kind = kind_ref[s]
    _stage_weights((w_hbm,), stage, wb, sem, kind, s, e_ref, c_ref, nxe_ref, nxc_ref, tn)

    @pl.when(kind >= ITEM_COMPUTE)
    def _():
        o_ref[...] = jnp.dot(a_ref[...], wb[0], preferred_element_type=F32).astype(o_ref.dtype)

    @pl.when(kind == ITEM_ZERO)
    def _():
        o_ref[...] = jnp.zeros(o_ref.shape, o_ref.dtype)


def _expert_ffn(xs, ends, tiles_e, w1, w3, w2, tm_e):
    r, d = xs.shape
    _, _, f = w1.shape
    n_tiles = r // tm_e
    tf = _tile(f, 512)
    tn = _tile(d, 1024)
    row_map = lambda s, tile, col, *_: (tile[s], 0)
    out_map = lambda s, tile, col, *_: (tile[s], col[s])

    a_s = pl.pallas_call(
        functools.partial(_gm1_kernel, tf=tf),
        out_shape=jax.ShapeDtypeStruct((r, f), BF16),
        grid_spec=pltpu.PrefetchScalarGridSpec(
            num_scalar_prefetch=6,
            grid=(n_tiles * (f // tf),),
            in_specs=[pl.BlockSpec((tm_e, d), row_map),
                      pl.BlockSpec(memory_space=pl.ANY), pl.BlockSpec(memory_space=pl.ANY)],
            out_specs=pl.BlockSpec((tm_e, tf), out_map),
            scratch_shapes=[pltpu.VMEM((2, d, tf), F32), pltpu.VMEM((2, d, tf), BF16),
                            pltpu.SemaphoreType.DMA((2,))],
        ),
        compiler_params=_params(1),
        name="moe_expert_up",
    )(*_work_items(ends, tiles_e, n_tiles, f // tf), xs, w1, w3)

    return pl.pallas_call(
        functools.partial(_gm2_kernel, tn=tn),
        out_shape=jax.ShapeDtypeStruct((r, d), F32),
        grid_spec=pltpu.PrefetchScalarGridSpec(
            num_scalar_prefetch=6,
            grid=(n_tiles * (d // tn),),
            in_specs=[pl.BlockSpec((tm_e, f), row_map),
                      pl.BlockSpec(memory_space=pl.ANY)],
            out_specs=pl.BlockSpec((tm_e, tn), out_map),
            scratch_shapes=[pltpu.VMEM((1, f, tn), F32), pltpu.VMEM((1, f, tn), BF16),
                            pltpu.SemaphoreType.DMA((1,))],
        ),
        compiler_params=_params(1),
        name="moe_expert_down",
    )(*_work_items(ends, tiles_e, n_tiles, d // tn), a_s, w2)


def _combine_kernel(pos_ref, ys_hbm, x_ref, meta_ref, mod_ref, g_ref, o_ref, buf, sem, *, tm, n_steps):
    i = pl.program_id(0)

    def row_copy(half, r, k, src_row):
        return pltpu.make_async_copy(ys_hbm.at[pl.ds(src_row, 1), :], buf.at[half, k, pl.ds(r, 1), :],
                                     sem.at[half, k])

    def issue(step):
        half = step % 2
        base = step * tm

        def body(r, carry):
            for k in range(2):
                row_copy(half, r, k, pos_ref[2 * (base + r) + k]).start()
            return carry
        lax.fori_loop(0, tm, body, 0)

    @pl.when(i == 0)
    def _():
        issue(i)

    @pl.when(i + 1 < n_steps)
    def _():
        issue(i + 1)

    half = i % 2

    def wait(r, carry):
        for k in range(2):
            row_copy(half, r, k, 0).wait()
        return carry
    lax.fori_loop(0, tm, wait, 0)

    y = meta_ref[:, 4:5] * buf[half, 0] + meta_ref[:, 5:6] * buf[half, 1]
    xn = x_ref[...] + mod_ref[2:3, :] * y
    var = jnp.mean(xn * xn, axis=-1, keepdims=True)
    o_ref[...] = (xn * lax.rsqrt(var + EPS)) * g_ref[...]


def _combine_final(ys, pos, x, meta, mods, li, seq, g_final):
    t, d = x.shape
    tm = _tile(seq, 128, 8)
    return pl.pallas_call(
        functools.partial(_combine_kernel, tm=tm, n_steps=t // tm),
        out_shape=jax.ShapeDtypeStruct((t, d), F32),
        grid_spec=pltpu.PrefetchScalarGridSpec(
            num_scalar_prefetch=1,
            grid=(t // tm,),
            in_specs=[pl.BlockSpec(memory_space=pl.ANY),
                      pl.BlockSpec((tm, d), lambda m, pos_ref: (m, 0)),
                      pl.BlockSpec((tm, LANE), lambda m, pos_ref: (m, 0)),
                      pl.BlockSpec((None, None, 3, d), lambda m, pos_ref: (li, m // (seq // tm), 0, 0)),
                      pl.BlockSpec((1, d), lambda m, pos_ref: (0, 0))],
            out_specs=pl.BlockSpec((tm, d), lambda m, pos_ref: (m, 0)),
            scratch_shapes=[pltpu.VMEM((2, 2, tm, d), F32), pltpu.SemaphoreType.DMA((2, 2))],
        ),
        compiler_params=_params(1),
        name="moe_combine_final_norm",
    )(pos, ys, x, meta, mods, g_final.reshape(1, d))


def _rot_half_cols(w):
    half = w.shape[-1] // 2
    return jnp.concatenate([-w[..., half:], w[..., :half]], axis=-1)


def _prep_w_in(w_in, q_lora, kv_lora, conv_dim, seg0):
    d = w_in.shape[0]
    o1 = q_lora + kv_lora
    o2 = o1 + MLA_ROPE
    k_rope = w_in[:, o1:o2]
    head = jnp.concatenate([w_in[:, :o1], k_rope, _rot_half_cols(k_rope)], axis=1)
    pad = jnp.zeros((d, seg0 - head.shape[1]), w_in.dtype)
    return jnp.concatenate([head, pad, w_in[:, o2:]], axis=1).astype(BF16)


def _prep_w_uq(w_uq, heads):
    ql = w_uq.shape[0]
    w = w_uq.reshape(ql, heads, MLA_NOPE + MLA_ROPE)
    rope = w[..., MLA_NOPE:]
    return jnp.concatenate([w[..., :MLA_NOPE], rope, _rot_half_cols(rope)], axis=-1
                           ).reshape(ql, heads * QK_HEAD).astype(BF16)


def _prep_w_ukv(w_ukv, heads):
    kvl = w_ukv.shape[0]
    w = w_ukv.reshape(kvl, heads, MLA_NOPE + MLA_V)
    return jnp.concatenate([w[..., :MLA_NOPE].reshape(kvl, heads * MLA_NOPE),
                            w[..., MLA_NOPE:].reshape(kvl, heads * MLA_V)], axis=1).astype(BF16)


def kernel(x, c, positions, ada_w, ada_b, norm_g, even_w_in, even_q_norm_g, even_kv_norm_g, even_w_uq, even_w_ukv, even_conv_w, even_w_out, even_ffn_w1, even_ffn_w3, even_ffn_w2, odd_w_in, odd_conv_w, odd_conv_b, odd_gate_a_w, odd_gate_a_b, odd_gate_x_w, odd_gate_x_b, odd_lambda, odd_w_out, odd_router_w, odd_router_b, odd_exp_w1, odd_exp_w3, odd_exp_w2, final_norm_g):
    batch, seq, d = x.shape
    t = batch * seq
    q_lora = even_q_norm_g.shape[1]
    kv_lora = even_kv_norm_g.shape[1]
    heads = even_w_uq.shape[2] // (MLA_NOPE + MLA_ROPE)
    conv_dim = even_conv_w.shape[2]
    d_ff = even_ffn_w1.shape[2]
    n_exp = odd_router_w.shape[2]
    assert even_w_in.shape[0] == 1 and odd_w_in.shape[0] == 1, "one layer of each type"

    xf = x.reshape(t, d)
    mods = _ada_modulation(c, ada_w, ada_b)
    rt = _rope_table(positions)
    tm = _tile(seq, 1024, 16)

    tcv = _tile(conv_dim, 512)
    seg0 = _round_up(q_lora + kv_lora + LANE, tcv)
    w_in0 = _prep_w_in(even_w_in[0], q_lora, kv_lora, conv_dim, seg0)
    n_in0 = w_in0.shape[1]
    h = _norm_mod(xf, norm_g[0], mods, 0, seq)
    proj = _matmul([h], [w_in0], [0], _epi_store, [], [], [jax.ShapeDtypeStruct((t, n_in0), BF16)],
                   tm=tm, tn=_tile(n_in0, 1024), name="in_proj_0")[0]
    q = _q_projection(proj, even_q_norm_g[0], rt, _prep_w_uq(even_w_uq[0], heads), heads, q_lora)
    kn, v, kr = _kv_projection(proj, even_kv_norm_g[0], rt, _prep_w_ukv(even_w_ukv[0], heads),
                               heads, q_lora, kv_lora)
    attn = _attention(q, kn, kr, v, batch, seq, heads)
    conv = _gated_conv(proj, even_conv_w[0], seq, seg0, conv_dim)
    xf = _residual_matmul([attn, conv], even_w_out[0].astype(BF16), xf, mods, 0, seq,
                          tm=tm, tn=_tile(d, 512), name="out_proj_0")

    f_pad = _round_up(d_ff, 1024)
    w1 = jnp.pad(even_ffn_w1[0].astype(BF16), ((0, 0), (0, f_pad - d_ff)))
    w3 = jnp.pad(even_ffn_w3[0].astype(BF16), ((0, 0), (0, f_pad - d_ff)))
    w2 = jnp.pad(even_ffn_w2[0].astype(BF16), ((0, f_pad - d_ff), (0, 0)))
    h = _norm_mod(xf, norm_g[1], mods, 1, seq)
    act = _matmul([h], [w1, w3], [0, 0], _epi_swiglu, [], [], [jax.ShapeDtypeStruct((t, f_pad), BF16)],
                  tm=tm, tn=_tile(f_pad, 512), name="ffn_up")[0]
    xf = _residual_matmul([act], w2, xf, mods, 1, seq, tm=tm, tn=_tile(d, 1024),
                          tk=_tile(f_pad, 3072), name="ffn_down")

    width = odd_conv_w.shape[2]
    w_in1 = odd_w_in[0].astype(BF16)
    tn1 = _tile(width, 512)
    h = _norm_mod(xf, norm_g[2], mods, 2, seq)
    gate_br, xb_pre = _matmul([h], [w_in1, w_in1], [0, width // tn1], _epi_gelu_pair, [], [],
                              [jax.ShapeDtypeStruct((t, width), BF16)] * 2, tm=tm, tn=tn1,
                              name="in_proj_1")
    y = _rglru(xb_pre, gate_br, odd_conv_w[0], odd_conv_b[0], odd_gate_a_w[0].astype(BF16),
               odd_gate_a_b[0], odd_gate_x_w[0].astype(BF16), odd_gate_x_b[0], odd_lambda[0], batch, seq)
    xf = _residual_matmul([y], odd_w_out[0].astype(BF16), xf, mods, 2, seq, tm=tm, tn=_tile(d, 512),
                          name="out_proj_1")

    tm_e = _tile(seq, 512, 16)
    n_rows = 2 * t + n_exp * tm_e
    n_tiles = n_rows // tm_e
    h32, meta, cnt = _router(xf, norm_g[3], mods, 3, seq, odd_router_w[0], odd_router_b[0])
    counts = cnt[0, :n_exp].astype(jnp.int32)
    tiles_e = (counts + tm_e - 1) // tm_e
    ends = jnp.cumsum(tiles_e)
    start_rows = (ends - tiles_e) * tm_e
    e_idx = meta[:, 0:2].astype(jnp.int32)
    pos = (jnp.take(start_rows, e_idx) + meta[:, 2:4].astype(jnp.int32)).reshape(2 * t)
    tid = jnp.arange(n_tiles, dtype=jnp.int32)
    group_last = jnp.any((tid[:, None] == ends[None, :] - 1) & (tiles_e[None, :] > 0), axis=1)
    zero_flag = (group_last | (tid >= ends[-1])).astype(jnp.int32)
    xs = _dispatch(h32, pos, zero_flag, n_rows, tm_e, seq)
    ys = _expert_ffn(xs, ends, tiles_e, odd_exp_w1[0], odd_exp_w3[0], odd_exp_w2[0], tm_e)
    out = _combine_final(ys, pos, xf, meta, mods, 3, seq, final_norm_g)
    return out.reshape(batch, seq, d)
```

```python
import functools

import jax
import jax.numpy as jnp
from jax import lax
from jax.experimental import pallas as pl
from jax.experimental.pallas import tpu as pltpu

F32 = jnp.float32
BF16 = jnp.bfloat16

EPS = 1e-6
ROPE_THETA = 10000.0
LRU_C = 8.0
MLA_NOPE = 128
MLA_ROPE = 64
MLA_V = 128
QK_HEAD = MLA_NOPE + 2 * MLA_ROPE
LANE = 128
ADA_ROWS = 16
VMEM_LIMIT_BYTES = 56 * 1024 * 1024


def _params(grid_rank):
    return pltpu.CompilerParams(dimension_semantics=("arbitrary",) * grid_rank,
                                vmem_limit_bytes=VMEM_LIMIT_BYTES)


def _tile(dim, pref, mult=LANE):
    if dim <= pref:
        return dim
    t = (pref // mult) * mult
    while t > mult and dim % t:
        t -= mult
    assert dim % t == 0, (dim, pref)
    return t


def _round_up(x, m):
    return (x + m - 1) // m * m


CAST_ROWS = 256


def _cast_rows(src_ref, dst_ref):
    rows = src_ref.shape[0]
    step = CAST_ROWS if rows % CAST_ROWS == 0 else rows
    for r in range(0, rows, step):
        dst_ref[r:r + step, :] = src_ref[r:r + step, :].astype(dst_ref.dtype)


def _ada_kernel(c_ref, w_ref, b_ref, o_ref):
    c = c_ref[...]
    sc = (c * jax.nn.sigmoid(c)).astype(BF16)
    o_ref[...] = jnp.dot(sc, w_ref[...].astype(BF16), preferred_element_type=F32) + b_ref[...]


def _ada_modulation(c, ada_w, ada_b):
    nmod, d, d3 = ada_w.shape
    b = c.shape[0]
    tn = _tile(d3, 512)
    cp = jnp.zeros((ADA_ROWS, d), F32).at[:b].set(c)
    out = pl.pallas_call(
        _ada_kernel,
        out_shape=jax.ShapeDtypeStruct((nmod, ADA_ROWS, d3), F32),
        grid=(nmod, d3 // tn),
        in_specs=[
            pl.BlockSpec((ADA_ROWS, d), lambda i, j: (0, 0)),
            pl.BlockSpec((None, d, tn), lambda i, j: (i, 0, j)),
            pl.BlockSpec((None, 1, tn), lambda i, j: (i, 0, j)),
        ],
        out_specs=pl.BlockSpec((None, ADA_ROWS, tn), lambda i, j: (i, 0, j)),
        compiler_params=_params(2),
        name="ada_modulation",
    )(cp, ada_w, ada_b.reshape(nmod, 1, d3))
    return out[:, :b].reshape(nmod, b, 3, d)


def _mod_spec(li, d_blk, rows_per_batch_tiles, col_map=None):
    if col_map is None:
        return pl.BlockSpec((None, None, 3, d_blk),
                            lambda m, *_: (li, m // rows_per_batch_tiles, 0, 0))
    return pl.BlockSpec((None, None, 3, d_blk),
                        lambda m, n, *_: (li, m // rows_per_batch_tiles, 0, n))


NORM_CHUNK = 16


def _norm_rows(x, gain, shift):
    var = jnp.mean(x * x, axis=-1, keepdims=True)
    return (x * lax.rsqrt(var + EPS)) * gain + shift


def _norm_kernel(x_ref, g_ref, mod_ref, h_ref, *, tm):
    gain = g_ref[...] * (1.0 + mod_ref[1:2, :])
    shift = mod_ref[0:1, :]

    def body(r, carry):
        r0 = pl.multiple_of(r * NORM_CHUNK, NORM_CHUNK)
        x = x_ref[pl.ds(r0, NORM_CHUNK), :]
        h_ref[pl.ds(r0, NORM_CHUNK), :] = _norm_rows(x, gain, shift).astype(h_ref.dtype)
        return carry
    lax.fori_loop(0, tm // NORM_CHUNK, body, 0, unroll=2)


def _norm_mod(x, g, mods, li, seq):
    t, d = x.shape
    tm = _tile(seq, 512, NORM_CHUNK)
    return pl.pallas_call(
        functools.partial(_norm_kernel, tm=tm),
        out_shape=jax.ShapeDtypeStruct((t, d), BF16),
        grid=(t // tm,),
        in_specs=[
            pl.BlockSpec((tm, d), lambda m: (m, 0)),
            pl.BlockSpec((1, d), lambda m: (0, 0)),
            _mod_spec(li, d, seq // tm),
        ],
        out_specs=pl.BlockSpec((tm, d), lambda m: (m, 0)),
        compiler_params=_params(1),
        name=f"norm_mod_{li}",
    )(x, g.reshape(1, d), mods)


def _mm_kernel(*refs, n_lhs, n_w, n_extra, n_out, nk, epilogue):
    lhs = refs[:n_lhs]
    ws = refs[n_lhs:n_lhs + n_w]
    extra = refs[n_lhs + n_w:n_lhs + n_w + n_extra]
    outs = refs[n_lhs + n_w + n_extra:n_lhs + n_w + n_extra + n_out]
    accs = refs[n_lhs + n_w + n_extra + n_out:]

    def partial_product(w_ref):
        off, tot = 0, None
        for l_ref in lhs:
            kk = l_ref.shape[1]
            part = jnp.dot(l_ref[...], w_ref[off:off + kk, :], preferred_element_type=F32)
            tot = part if tot is None else tot + part
            off += kk
        return tot

    if nk == 1:
        _run_epilogue(epilogue, [partial_product(w) for w in ws], extra, outs)
        return

    k = pl.program_id(2)

    @pl.when(k == 0)
    def _():
        for acc, w in zip(accs, ws):
            acc[...] = partial_product(w)

    @pl.when(k > 0)
    def _():
        for acc, w in zip(accs, ws):
            acc[...] += partial_product(w)

    @pl.when(k == nk - 1)
    def _():
        _run_epilogue(epilogue, [acc[...] for acc in accs], extra, outs)


def _matmul(lhs_list, w_list, w_col_offsets, epilogue, extras, extra_specs, out_shapes,
            *, tm, tn, tk=None, name):
    t = lhs_list[0].shape[0]
    ktot = sum(l.shape[1] for l in lhs_list)
    n = out_shapes[0].shape[1]
    if tk is None:
        nk = 1
        lhs_specs = [pl.BlockSpec((tm, l.shape[1]), lambda m, j, k: (m, 0)) for l in lhs_list]
        w_rows = ktot
    else:
        assert len(lhs_list) == 1 and ktot % tk == 0
        nk = ktot // tk
        lhs_specs = [pl.BlockSpec((tm, tk), lambda m, j, k: (m, k))]
        w_rows = tk
    w_specs = [pl.BlockSpec((w_rows, tn), functools.partial(lambda m, j, k, off: (k, j + off), off=off))
               for off in w_col_offsets]
    kern = functools.partial(_mm_kernel, n_lhs=len(lhs_list), n_w=len(w_list), n_extra=len(extras),
                             n_out=len(out_shapes), nk=nk, epilogue=epilogue)
    scratch = [pltpu.VMEM((tm, tn), F32) for _ in w_list] if nk > 1 else []
    return pl.pallas_call(
        kern,
        out_shape=out_shapes,
        grid=(t // tm, n // tn, nk),
        in_specs=lhs_specs + w_specs + list(extra_specs),
        out_specs=[pl.BlockSpec((tm, tn), lambda m, j, k: (m, j)) for _ in out_shapes],
        scratch_shapes=scratch,
        compiler_params=_params(3),
        name=name,
    )(*lhs_list, *w_list, *extras)


def _ws_kernel(*refs, n_lhs, n_w, n_extra, n_out, tn, nj, col_offsets, rem, epilogue):
    lhs = refs[:n_lhs]
    w_hbm = refs[n_lhs:n_lhs + n_w]
    extra = refs[n_lhs + n_w:n_lhs + n_w + n_extra]
    outs = refs[n_lhs + n_w + n_extra:n_lhs + n_w + n_extra + n_out]
    stage, wb, sem = refs[n_lhs + n_w + n_extra + n_out:]
    j = pl.program_id(0)
    m = pl.program_id(1)

    def wcopy(col_tile, k, width):
        cols = pl.ds(pl.multiple_of(col_tile * tn + col_offsets[k], LANE), width)
        dst = stage.at[k] if width == tn else stage.at[k, :, 0:width]
        return pltpu.make_async_copy(w_hbm[k].at[0, :, cols], dst, sem.at[k])

    def for_tile(col_tile, fn):
        if rem == tn:
            for k in range(n_w):
                fn(wcopy(col_tile, k, tn))
            return

        @pl.when(col_tile < nj - 1)
        def _():
            for k in range(n_w):
                fn(wcopy(col_tile, k, tn))

        @pl.when(col_tile == nj - 1)
        def _():
            for k in range(n_w):
                fn(wcopy(col_tile, k, rem))

    @pl.when((j == 0) & (m == 0))
    def _():
        for_tile(j, lambda c: c.start())

    @pl.when(m == 0)
    def _():
        for_tile(j, lambda c: c.wait())
        for k in range(n_w):
            _cast_rows(stage.at[k], wb.at[k])
        if rem != tn:
            @pl.when(j == nj - 1)
            def _():
                for k in range(n_w):
                    wb[k, :, rem:] = jnp.zeros((wb.shape[1], tn - rem), BF16)

        @pl.when(j + 1 < nj)
        def _():
            for_tile(j + 1, lambda c: c.start())

    accs = []
    for k in range(n_w):
        off, tot = 0, None
        for l_ref in lhs:
            kk = l_ref.shape[1]
            part = jnp.dot(l_ref[...], wb[k, off:off + kk, :], preferred_element_type=F32)
            tot = part if tot is None else tot + part
            off += kk
        accs.append(tot)
    _run_epilogue(epilogue, accs, extra, outs)


def _ws_matmul(lhs_list, w_list, col_offsets, n_valid, epilogue, extras, extra_specs, out_shapes,
               *, tm, tn, name):
    t = lhs_list[0].shape[0]
    ktot = sum(l.shape[1] for l in lhs_list)
    n_out = out_shapes[0].shape[1]
    nj = n_out // tn
    rem = n_valid - (nj - 1) * tn
    assert 0 < rem <= tn and (rem == tn or nj > 1)
    kern = functools.partial(_ws_kernel, n_lhs=len(lhs_list), n_w=len(w_list), n_extra=len(extras),
                             n_out=len(out_shapes), tn=tn, nj=nj, col_offsets=tuple(col_offsets), rem=rem,
                             epilogue=epilogue)
    return pl.pallas_call(
        kern,
        out_shape=out_shapes,
        grid=(nj, t // tm),
        in_specs=[pl.BlockSpec((tm, l.shape[1]), lambda j, m: (m, 0)) for l in lhs_list]
                 + [pl.BlockSpec(memory_space=pl.ANY) for _ in w_list] + list(extra_specs),
        out_specs=[pl.BlockSpec((tm, tn), lambda j, m: (m, j)) for _ in out_shapes],
        scratch_shapes=[pltpu.VMEM((len(w_list), ktot, tn), F32), pltpu.VMEM((len(w_list), ktot, tn), BF16),
                        pltpu.SemaphoreType.DMA((len(w_list),))],
        compiler_params=_params(2),
        name=name,
    )(*lhs_list, *w_list, *extras)


def _ws_residual_matmul(lhs_list, w, x, mods, li, seq, *, tm, tn, name):
    t, d = x.shape
    tiles_per_batch = seq // tm
    extras = [x, mods]
    extra_specs = [pl.BlockSpec((tm, tn), lambda j, m: (m, j)),
                   pl.BlockSpec((None, None, 3, tn), lambda j, m: (li, m // tiles_per_batch, 0, j))]
    return _ws_matmul(lhs_list, [w], [0], d, _epi_residual, extras, extra_specs,
                      [jax.ShapeDtypeStruct((t, d), F32)], tm=tm, tn=tn, name=name)[0]


EPILOGUE_ROWS = 128


def _run_epilogue(epilogue, accs, extra, outs):
    rows = accs[0].shape[0]
    step = EPILOGUE_ROWS if rows % EPILOGUE_ROWS == 0 else rows
    for r in range(0, rows, step):
        epilogue([a[r:r + step] for a in accs], extra, outs, slice(r, r + step))


def _epi_store(accs, extra, outs, rs):
    outs[0][rs, :] = accs[0].astype(outs[0].dtype)


def _epi_residual(accs, extra, outs, rs):
    x_ref, mod_ref = extra
    outs[0][rs, :] = x_ref[rs, :] + mod_ref[2:3, :] * accs[0]


def _epi_swiglu(accs, extra, outs, rs):
    a, b = accs
    outs[0][rs, :] = (jax.nn.silu(a) * b).astype(outs[0].dtype)


def _epi_gelu_pair(accs, extra, outs, rs):
    a, b = accs
    outs[0][rs, :] = jax.nn.gelu(a, approximate=True).astype(outs[0].dtype)
    outs[1][rs, :] = b.astype(outs[1].dtype)


def _residual_matmul(lhs_list, w, x, mods, li, seq, *, tm, tn, tk=None, name):
    t, d = x.shape
    extras = [x, mods]
    extra_specs = [pl.BlockSpec((tm, tn), lambda m, j, k: (m, j)),
                   _mod_spec(li, tn, seq // tm, col_map=True)]
    return _matmul(lhs_list, [w], [0], _epi_residual, extras, extra_specs,
                   [jax.ShapeDtypeStruct((t, d), F32)], tm=tm, tn=tn, tk=tk, name=name)[0]


def _rope_kernel(pos_ref, inv_ref, o_ref):
    ang = pos_ref[...].astype(F32) * inv_ref[...]
    lane = lax.broadcasted_iota(jnp.int32, ang.shape, 1)
    o_ref[...] = jnp.where(lane < MLA_ROPE, jnp.cos(ang), jnp.sin(ang))


def _rope_table(positions):
    t = positions.size
    tm = _tile(t, 1024, 8)
    half = MLA_ROPE // 2
    inv = 1.0 / (ROPE_THETA ** (jnp.arange(0, MLA_ROPE, 2, dtype=F32) / MLA_ROPE))
    inv4 = jnp.tile(inv, 4).reshape(1, 4 * half)
    return pl.pallas_call(
        _rope_kernel,
        out_shape=jax.ShapeDtypeStruct((t, LANE), F32),
        grid=(t // tm,),
        in_specs=[pl.BlockSpec((tm, 1), lambda m: (m, 0)),
                  pl.BlockSpec((1, LANE), lambda m: (0, 0))],
        out_specs=pl.BlockSpec((tm, LANE), lambda m: (m, 0)),
        compiler_params=_params(1),
        name="rope_table",
    )(positions.reshape(t, 1), inv4)


def _qproj_kernel(cq_ref, g_ref, rt_ref, w_ref, o_ref, nq_ref, *, heads_per_tile, scale):
    @pl.when(pl.program_id(1) == 0)
    def _():
        cq = cq_ref[...].astype(F32)
        var = jnp.mean(cq * cq, axis=-1, keepdims=True)
        nq_ref[...] = ((cq * lax.rsqrt(var + EPS)) * g_ref[...]).astype(BF16)

    res = jnp.dot(nq_ref[...], w_ref[...], preferred_element_type=F32)
    rt = rt_ref[...] * scale
    for hh in range(heads_per_tile):
        c0 = hh * QK_HEAD
        o_ref[:, c0:c0 + MLA_NOPE] = (res[:, c0:c0 + MLA_NOPE] * scale).astype(o_ref.dtype)
        o_ref[:, c0 + MLA_NOPE:c0 + QK_HEAD] = (res[:, c0 + MLA_NOPE:c0 + QK_HEAD] * rt).astype(o_ref.dtype)


def _q_projection(proj, g_q, rt, w_uq_p, heads, q_lora):
    t = proj.shape[0]
    tm = _tile(t, 512, 16)
    hpt = min(heads, 4)
    tn = hpt * QK_HEAD
    scale = (MLA_NOPE + MLA_ROPE) ** -0.5
    return pl.pallas_call(
        functools.partial(_qproj_kernel, heads_per_tile=hpt, scale=scale),
        out_shape=jax.ShapeDtypeStruct((t, heads * QK_HEAD), BF16),
        grid=(t // tm, heads // hpt),
        in_specs=[
            pl.BlockSpec((tm, q_lora), lambda m, j: (m, 0)),
            pl.BlockSpec((1, q_lora), lambda m, j: (0, 0)),
            pl.BlockSpec((tm, LANE), lambda m, j: (m, 0)),
            pl.BlockSpec((q_lora, tn), lambda m, j: (0, j)),
        ],
        out_specs=pl.BlockSpec((tm, tn), lambda m, j: (m, j)),
        scratch_shapes=[pltpu.VMEM((tm, q_lora), BF16)],
        compiler_params=_params(2),
        name="q_projection",
    )(proj, g_q.reshape(1, q_lora), rt, w_uq_p)


def _kvproj_kernel(ckv_ref, kr_ref, g_ref, rt_ref, w_ref, kn_ref, v_ref, kro_ref, *, hv):
    ckv = ckv_ref[...].astype(F32)
    var = jnp.mean(ckv * ckv, axis=-1, keepdims=True)
    nkv = ((ckv * lax.rsqrt(var + EPS)) * g_ref[...]).astype(BF16)
    res = jnp.dot(nkv, w_ref[...], preferred_element_type=F32)
    kn_ref[...] = res[:, :hv].astype(kn_ref.dtype)
    v_ref[...] = res[:, hv:].astype(v_ref.dtype)
    kv = kr_ref[...].astype(F32) * rt_ref[...]
    kro_ref[...] = (kv + pltpu.roll(kv, MLA_ROPE, 1)).astype(kro_ref.dtype)


def _kv_projection(proj, g_kv, rt, w_ukv_p, heads, q_lora, kv_lora):
    t = proj.shape[0]
    tm = _tile(t, 512, 16)
    hv = heads * MLA_NOPE
    return pl.pallas_call(
        functools.partial(_kvproj_kernel, hv=hv),
        out_shape=[jax.ShapeDtypeStruct((t, hv), BF16), jax.ShapeDtypeStruct((t, hv), BF16),
                   jax.ShapeDtypeStruct((t, LANE), BF16)],
        grid=(t // tm,),
        in_specs=[
            pl.BlockSpec((tm, kv_lora), lambda m: (m, q_lora // kv_lora)),
            pl.BlockSpec((tm, LANE), lambda m: (m, (q_lora + kv_lora) // LANE)),
            pl.BlockSpec((1, kv_lora), lambda m: (0, 0)),
            pl.BlockSpec((tm, LANE), lambda m: (m, 0)),
            pl.BlockSpec((kv_lora, 2 * hv), lambda m: (0, 0)),
        ],
        out_specs=[pl.BlockSpec((tm, hv), lambda m: (m, 0)), pl.BlockSpec((tm, hv), lambda m: (m, 0)),
                   pl.BlockSpec((tm, LANE), lambda m: (m, 0))],
        compiler_params=_params(1),
        name="kv_projection",
    )(proj, proj, g_kv.reshape(1, kv_lora), rt, w_ukv_p)


def _attn_kernel(q_ref, kn_ref, kr_ref, v_ref, o_ref, *, tq, hpb):
    qi = pl.program_id(2)
    ones = jnp.ones((tq, MLA_V), BF16)

    def block(ki, carry, masked):
        ks = pl.multiple_of(ki * tq, tq)
        kr = kr_ref[pl.ds(ks, tq), :]
        new = []
        for hh in range(hpb):
            m, acc = carry[hh]
            q = q_ref[:, hh * QK_HEAD:(hh + 1) * QK_HEAD]
            k = jnp.concatenate([kn_ref[pl.ds(ks, tq), hh * MLA_NOPE:(hh + 1) * MLA_NOPE], kr], axis=1)
            s = lax.dot_general(q, k, (((1,), (1,)), ((), ())), preferred_element_type=F32)
            if masked:
                row = lax.broadcasted_iota(jnp.int32, s.shape, 0)
                col = lax.broadcasted_iota(jnp.int32, s.shape, 1)
                s = jnp.where(col <= row, s, -jnp.inf)
            m_new = jnp.maximum(m, jnp.max(s, axis=1, keepdims=True))
            alpha = jnp.exp(m - m_new)
            p = jnp.exp(s - m_new).astype(BF16)
            v1 = jnp.concatenate([v_ref[pl.ds(ks, tq), hh * MLA_V:(hh + 1) * MLA_V], ones], axis=1)
            acc = alpha * acc + jnp.dot(p, v1, preferred_element_type=F32)
            new.append((m_new, acc))
        return tuple(new)

    init = tuple((jnp.full((tq, 1), -jnp.inf, F32), jnp.zeros((tq, 2 * MLA_V), F32)) for _ in range(hpb))
    carry = lax.fori_loop(0, qi, lambda ki, c: block(ki, c, False), init)
    carry = block(qi, carry, True)
    for hh in range(hpb):
        acc = carry[hh][1]
        o_ref[:, hh * MLA_V:(hh + 1) * MLA_V] = (acc[:, :MLA_V] / acc[:, MLA_V:]).astype(o_ref.dtype)


def _attention(q, kn, kr, v, batch, seq, heads):
    t = q.shape[0]
    tq = _tile(seq, 512, 16)
    nq = seq // tq
    hpb = 2 if heads % 2 == 0 else 1
    return pl.pallas_call(
        functools.partial(_attn_kernel, tq=tq, hpb=hpb),
        out_shape=jax.ShapeDtypeStruct((t, heads * MLA_V), BF16),
        grid=(batch, heads // hpb, nq),
        in_specs=[
            pl.BlockSpec((tq, hpb * QK_HEAD), lambda b, h, i: (b * nq + i, h)),
            pl.BlockSpec((seq, hpb * MLA_NOPE), lambda b, h, i: (b, h)),
            pl.BlockSpec((seq, LANE), lambda b, h, i: (b, 0)),
            pl.BlockSpec((seq, hpb * MLA_V), lambda b, h, i: (b, h)),
        ],
        out_specs=pl.BlockSpec((tq, hpb * MLA_V), lambda b, h, i: (b * nq + i, h)),
        compiler_params=_params(3),
        name="mla_attention",
    )(q, kn, kr, v)


HALO = 16


def _conv3_kernel(gb_ref, gc_ref, u_ref, gch_ref, uh_ref, w_ref, o_ref, ext_ref, *, tm, tiles_per_seq, taps):
    first = (pl.program_id(0) % tiles_per_seq) == 0
    halo = gch_ref[...].astype(F32) * uh_ref[...].astype(F32)
    ext_ref[0:HALO, :] = jnp.where(first, 0.0, halo)
    ext_ref[HALO:, :] = gc_ref[...].astype(F32) * u_ref[...].astype(F32)
    y = None
    for j in range(taps):
        off = HALO - (taps - 1 - j)
        term = w_ref[j:j + 1, :] * ext_ref[off:off + tm, :]
        y = term if y is None else y + term
    o_ref[...] = (gb_ref[...].astype(F32) * y).astype(o_ref.dtype)


def _gated_conv(proj, conv_w, seq, conv_dim):
    t = proj.shape[0]
    taps = conv_w.shape[0]
    tm = _tile(seq, 512, HALO)
    tc = _tile(conv_dim, 512)
    ob, oc, ou = 0, conv_dim // tc, 2 * conv_dim // tc
    hb = tm // HALO

    def halo_map(off):
        return lambda m, c: (jnp.maximum(m * hb - 1, 0), off + c)

    return pl.pallas_call(
        functools.partial(_conv3_kernel, tm=tm, tiles_per_seq=seq // tm, taps=taps),
        out_shape=jax.ShapeDtypeStruct((t, conv_dim), BF16),
        grid=(t // tm, conv_dim // tc),
        in_specs=[
            pl.BlockSpec((tm, tc), lambda m, c: (m, ob + c)),
            pl.BlockSpec((tm, tc), lambda m, c: (m, oc + c)),
            pl.BlockSpec((tm, tc), lambda m, c: (m, ou + c)),
            pl.BlockSpec((HALO, tc), halo_map(oc)),
            pl.BlockSpec((HALO, tc), halo_map(ou)),
            pl.BlockSpec((taps, tc), lambda m, c: (0, c)),
        ],
        out_specs=pl.BlockSpec((tm, tc), lambda m, c: (m, c)),
        scratch_shapes=[pltpu.VMEM((tm + HALO, tc), F32)],
        compiler_params=_params(2),
        name="gated_conv3",
    )(proj, proj, proj, proj, proj, conv_w)


GROUP = 8


def _lru_kernel(xb_ref, gbr_ref, cw_ref, cb_ref, wa_ref, ba_ref, wx_ref, bx_ref, lam_ref, o_ref,
                ext_ref, a_ref, b_ref, hc_ref, *, tm, hd, hp, taps):
    @pl.when(pl.program_id(2) == 0)
    def _():
        ext_ref[0:GROUP, :] = jnp.zeros((GROUP, ext_ref.shape[1]), F32)
        hc_ref[...] = jnp.zeros(hc_ref.shape, F32)

    ext_ref[GROUP:, :] = xb_ref[...].astype(F32)
    xb = cb_ref[...]
    for j in range(taps):
        off = GROUP - (taps - 1 - j)
        xb = xb + cw_ref[j:j + 1, :] * ext_ref[off:off + tm, :]
    ext_ref[0:GROUP, :] = ext_ref[tm:tm + GROUP, :]

    xb16 = xb.astype(BF16)
    rs, is_ = [], []
    for h in range(hp):
        xh = xb16[:, h * hd:(h + 1) * hd]
        rs.append(jnp.dot(xh, wa_ref[h], preferred_element_type=F32))
        is_.append(jnp.dot(xh, wx_ref[h], preferred_element_type=F32))
    r = jax.nn.sigmoid(jnp.concatenate(rs, axis=1) + ba_ref[...])
    ig = jax.nn.sigmoid(jnp.concatenate(is_, axis=1) + bx_ref[...])
    lam = lam_ref[...]
    log_sig = -(jnp.maximum(-lam, 0.0) + jnp.log(1.0 + jnp.exp(-jnp.abs(lam))))
    log_a = (LRU_C * r) * log_sig
    a = jnp.exp(log_a)
    th = jnp.tanh(log_a)
    mult = jnp.sqrt(-2.0 * th / (1.0 - th))
    a_ref[...] = a
    b_ref[...] = mult * (ig * xb)

    rowg = lax.broadcasted_iota(jnp.int32, (GROUP, a_ref.shape[1]), 0)

    def group(gi, hc):
        r0 = pl.multiple_of(gi * GROUP, GROUP)
        ag = a_ref[pl.ds(r0, GROUP), :]
        bg = b_ref[pl.ds(r0, GROUP), :]
        for dist in (1, 2, 4):
            keep = rowg >= dist
            ap = jnp.where(keep, pltpu.roll(ag, dist, 0), 1.0)
            bp = jnp.where(keep, pltpu.roll(bg, dist, 0), 0.0)
            bg = ag * bp + bg
            ag = ag * ap
        hs = ag * hc + bg
        y = hs * gbr_ref[pl.ds(r0, GROUP), :].astype(F32)
        b_ref[pl.ds(r0, GROUP), :] = y
        return jnp.broadcast_to(hs[GROUP - 1:GROUP, :], hs.shape)

    hc_ref[...] = lax.fori_loop(0, tm // GROUP, group, hc_ref[...])
    o_ref[...] = b_ref[...].astype(o_ref.dtype)


def _rglru(xb_pre, gate_br, conv_w, conv_b, wa, ba, wx, bx, lam, batch, seq):
    t, width = xb_pre.shape
    heads, hd, _ = wa.shape
    taps = conv_w.shape[0]
    hp = 2 if heads % 2 == 0 else 1
    c = hp * hd
    tm = _tile(seq, 256, 16)
    nt = seq // tm
    row = lambda b, h, i: (b * nt + i, h)
    vec = lambda b, h, i: (0, h)
    return pl.pallas_call(
        functools.partial(_lru_kernel, tm=tm, hd=hd, hp=hp, taps=taps),
        out_shape=jax.ShapeDtypeStruct((t, width), BF16),
        grid=(batch, heads // hp, nt),
        in_specs=[
            pl.BlockSpec((tm, c), row),
            pl.BlockSpec((tm, c), row),
            pl.BlockSpec((taps, c), vec),
            pl.BlockSpec((1, c), vec),
            pl.BlockSpec((hp, hd, hd), lambda b, h, i: (h, 0, 0)),
            pl.BlockSpec((1, c), vec),
            pl.BlockSpec((hp, hd, hd), lambda b, h, i: (h, 0, 0)),
            pl.BlockSpec((1, c), vec),
            pl.BlockSpec((1, c), vec),
        ],
        out_specs=pl.BlockSpec((tm, c), row),
        scratch_shapes=[pltpu.VMEM((tm + GROUP, c), F32), pltpu.VMEM((tm, c), F32),
                        pltpu.VMEM((tm, c), F32), pltpu.VMEM((GROUP, c), F32)],
        compiler_params=_params(3),
        name="rglru",
    )(xb_pre, gate_br, conv_w, conv_b.reshape(1, width), wa, ba.reshape(1, width), wx,
      bx.reshape(1, width), lam.reshape(1, width))


def _router_kernel(x_ref, g_ref, mod_ref, rw_ref, rb_ref, h_ref, meta_ref, cnt_ref, run_ref, hi_ref, lo_ref,
                   *, tm, n_exp):
    @pl.when(pl.program_id(0) == 0)
    def _():
        run_ref[...] = jnp.zeros(run_ref.shape, F32)

    gain = g_ref[...] * (1.0 + mod_ref[1:2, :])
    shift = mod_ref[0:1, :]

    def body(r, carry):
        r0 = pl.multiple_of(r * NORM_CHUNK, NORM_CHUNK)
        h = _norm_rows(x_ref[pl.ds(r0, NORM_CHUNK), :], gain, shift)
        hi = h.astype(BF16)
        h_ref[pl.ds(r0, NORM_CHUNK), :] = h
        hi_ref[pl.ds(r0, NORM_CHUNK), :] = hi
        lo_ref[pl.ds(r0, NORM_CHUNK), :] = (h - hi.astype(F32)).astype(BF16)
        return carry
    lax.fori_loop(0, tm // NORM_CHUNK, body, 0, unroll=2)

    logits = (jnp.dot(hi_ref[...], rw_ref[0], preferred_element_type=F32)
              + jnp.dot(lo_ref[...], rw_ref[0], preferred_element_type=F32)
              + jnp.dot(hi_ref[...], rw_ref[1], preferred_element_type=F32)) + rb_ref[...]
    lane = lax.broadcasted_iota(jnp.int32, logits.shape, 1)
    lg = jnp.where(lane < n_exp, logits, -jnp.inf)
    m1 = jnp.max(lg, axis=1, keepdims=True)
    i1 = jnp.min(jnp.where(lg == m1, lane, LANE), axis=1, keepdims=True)
    lg2 = jnp.where(lane == i1, -jnp.inf, lg)
    m2 = jnp.max(lg2, axis=1, keepdims=True)
    i2 = jnp.min(jnp.where(lg2 == m2, lane, LANE), axis=1, keepdims=True)
    e2 = jnp.exp(m2 - m1)
    den = 1.0 + e2
    w1 = 1.0 / den
    w2 = e2 / den

    hit1 = lane == i1
    hit2 = lane == i2
    sel = jnp.where(hit1 | hit2, 1.0, 0.0)
    rowi = lax.broadcasted_iota(jnp.int32, (tm, tm), 0)
    coli = lax.broadcasted_iota(jnp.int32, (tm, tm), 1)
    earlier = jnp.where(coli < rowi, 1.0, 0.0).astype(BF16)
    rank = jnp.dot(earlier, sel.astype(BF16), preferred_element_type=F32) + run_ref[0:1, :]
    r1 = jnp.sum(jnp.where(hit1, rank, 0.0), axis=1, keepdims=True)
    r2 = jnp.sum(jnp.where(hit2, rank, 0.0), axis=1, keepdims=True)
    run_ref[...] = run_ref[...] + jnp.sum(sel, axis=0, keepdims=True)
    cnt_ref[...] = run_ref[...]

    meta = jnp.where(lane == 0, i1.astype(F32), 0.0)
    meta = jnp.where(lane == 1, i2.astype(F32), meta)
    meta = jnp.where(lane == 2, r1, meta)
    meta = jnp.where(lane == 3, r2, meta)
    meta = jnp.where(lane == 4, w1, meta)
    meta = jnp.where(lane == 5, w2, meta)
    meta_ref[...] = meta


def _router(x, g, mods, li, seq, router_w, router_b):
    t, d = x.shape
    n_exp = router_w.shape[1]
    tm = _tile(seq, 256, 16)
    rw = jnp.zeros((d, LANE), F32).at[:, :n_exp].set(router_w)
    rw_hi = rw.astype(BF16)
    rw = jnp.stack([rw_hi, (rw - rw_hi.astype(F32)).astype(BF16)])
    rb = jnp.zeros((1, LANE), F32).at[0, :n_exp].set(router_b)
    return pl.pallas_call(
        functools.partial(_router_kernel, tm=tm, n_exp=n_exp),
        out_shape=[jax.ShapeDtypeStruct((t, d), F32), jax.ShapeDtypeStruct((t, LANE), F32),
                   jax.ShapeDtypeStruct((GROUP, LANE), F32)],
        grid=(t // tm,),
        in_specs=[
            pl.BlockSpec((tm, d), lambda m: (m, 0)),
            pl.BlockSpec((1, d), lambda m: (0, 0)),
            _mod_spec(li, d, seq // tm),
            pl.BlockSpec((2, d, LANE), lambda m: (0, 0, 0)),
            pl.BlockSpec((1, LANE), lambda m: (0, 0)),
        ],
        out_specs=[pl.BlockSpec((tm, d), lambda m: (m, 0)), pl.BlockSpec((tm, LANE), lambda m: (m, 0)),
                   pl.BlockSpec((GROUP, LANE), lambda m: (0, 0))],
        scratch_shapes=[pltpu.VMEM((GROUP, LANE), F32), pltpu.VMEM((tm, d), BF16), pltpu.VMEM((tm, d), BF16)],
        compiler_params=_params(1),
        name="moe_router",
    )(x, g.reshape(1, d), mods, rw, rb)


def _dispatch_kernel(pos_ref, zf_ref, h_ref, xs_hbm, zbuf, zsem, sem, *, tm, n_tiles, zr):
    @pl.when(pl.program_id(0) == 0)
    def _():
        zbuf[...] = jnp.zeros(zbuf.shape, zbuf.dtype)

        def zero_copy(c):
            return pltpu.make_async_copy(zbuf, xs_hbm.at[pl.ds(c * zr, zr), :], zsem)

        def zstart(c, carry):
            @pl.when(zf_ref[c] > 0)
            def _():
                zero_copy(c).start()
            return carry
        lax.fori_loop(0, n_tiles, zstart, 0)

        def zwait(c, carry):
            @pl.when(zf_ref[c] > 0)
            def _():
                zero_copy(c).wait()
            return carry
        lax.fori_loop(0, n_tiles, zwait, 0)

    base = pl.program_id(0) * tm

    def row_copy(r, slot):
        return pltpu.make_async_copy(h_ref.at[pl.ds(r, 1), :], xs_hbm.at[pl.ds(slot, 1), :], sem)

    def issue(r, carry):
        row_copy(r, pos_ref[2 * (base + r)]).start()
        row_copy(r, pos_ref[2 * (base + r) + 1]).start()
        return carry
    lax.fori_loop(0, tm, issue, 0)

    def wait(r, carry):
        row_copy(r, 0).wait()
        row_copy(r, 0).wait()
        return carry
    lax.fori_loop(0, tm, wait, 0)


def _dispatch(h, pos, zero_flag, n_rows, zr, seq):
    t, d = h.shape
    tm = _tile(seq, 256, 8)
    n_tiles = n_rows // zr
    return pl.pallas_call(
        functools.partial(_dispatch_kernel, tm=tm, n_tiles=n_tiles, zr=zr),
        out_shape=jax.ShapeDtypeStruct((n_rows, d), h.dtype),
        grid_spec=pltpu.PrefetchScalarGridSpec(
            num_scalar_prefetch=2,
            grid=(t // tm,),
            in_specs=[pl.BlockSpec((tm, d), lambda m, pos_ref, zf_ref: (m, 0))],
            out_specs=pl.BlockSpec(memory_space=pl.ANY),
            scratch_shapes=[pltpu.VMEM((zr, d), h.dtype), pltpu.SemaphoreType.DMA(()),
                            pltpu.SemaphoreType.DMA(())],
        ),
        compiler_params=_params(1),
        name="moe_dispatch",
    )(pos, zero_flag, h)


ITEM_ZERO, ITEM_COMPUTE, ITEM_FIRST, ITEM_FIRST_MORE = 0, 1, 2, 3


def _work_items(ends, tiles_e, n_tiles, ncol):
    n_exp = ends.shape[0]
    s = jnp.arange(ncol * n_tiles, dtype=jnp.int32)
    n_valid = ncol * ends[-1]
    e = jnp.minimum(jnp.sum((s[:, None] >= ncol * ends[None, :]).astype(jnp.int32), axis=1), n_exp - 1)
    te = jnp.maximum(jnp.take(tiles_e, e), 1)
    local = s - ncol * jnp.take(ends - tiles_e, e)
    col = local // te
    tile = jnp.take(ends - tiles_e, e) + local % te
    valid = s < n_valid
    first = valid & (local % te == 0)
    nxt = jnp.minimum(s + te, ncol * n_tiles - 1)
    more = (s + te) < n_valid
    rem = s - n_valid
    tile = jnp.where(valid, tile, ends[-1] + rem // ncol)
    col = jnp.where(valid, col, rem % ncol)
    kind = jnp.where(valid, jnp.where(first, jnp.where(more, ITEM_FIRST_MORE, ITEM_FIRST), ITEM_COMPUTE),
                     ITEM_ZERO)
    return tile, col, e, kind.astype(jnp.int32), jnp.take(e, nxt), jnp.take(col, nxt)


def _stage_weights(w_hbm_list, stage, wb, sem, kind, s, e_ref, c_ref, nxe_ref, nxc_ref, tn):
    def wcopy(e, c, k):
        cols = pl.ds(pl.multiple_of(c * tn, tn), tn)
        return pltpu.make_async_copy(w_hbm_list[k].at[e, :, cols], stage.at[k], sem.at[k])

    @pl.when(s == 0)
    def _():
        for k in range(len(w_hbm_list)):
            wcopy(e_ref[0], c_ref[0], k).start()

    @pl.when(kind >= ITEM_FIRST)
    def _():
        for k in range(len(w_hbm_list)):
            wcopy(e_ref[s], c_ref[s], k).wait()
            _cast_rows(stage.at[k], wb.at[k])

        @pl.when(kind == ITEM_FIRST_MORE)
        def _():
            for k in range(len(w_hbm_list)):
                wcopy(nxe_ref[s], nxc_ref[s], k).start()


def _gm1_kernel(tile_ref, c_ref, e_ref, kind_ref, nxe_ref, nxc_ref, x_ref, w1_hbm, w3_hbm, o_ref,
                stage, wb, sem, *, tf):
    s = pl.program_id(0)
    kind = kind_ref[s]
    _stage_weights((w1_hbm, w3_hbm), stage, wb, sem, kind, s, e_ref, c_ref, nxe_ref, nxc_ref, tf)

    @pl.when(kind >= ITEM_COMPUTE)
    def _():
        x = x_ref[...].astype(BF16)
        a = jnp.dot(x, wb[0], preferred_element_type=F32)
        b = jnp.dot(x, wb[1], preferred_element_type=F32)
        o_ref[...] = (jax.nn.silu(a) * b).astype(o_ref.dtype)

    @pl.when(kind == ITEM_ZERO)
    def _():
        o_ref[...] = jnp.zeros(o_ref.shape, o_ref.dtype)


def _gm2_kernel(tile_ref, c_ref, e_ref, kind_ref, nxe_ref, nxc_ref, a_ref, w_hbm, o_ref,
                stage, wb, sem, *, tn):
    s = pl.program_id(0)
    kind = kind_ref[s]
    _stage_weights((w_hbm,), stage, wb, sem, kind, s, e_ref, c_ref, nxe_ref, nxc_ref, tn)

    @pl.when(kind >= ITEM_COMPUTE)
    def _():
        o_ref[...] = jnp.dot(a_ref[...], wb[0], preferred_element_type=F32).astype(o_ref.dtype)

    @pl.when(kind == ITEM_ZERO)
    def _():
        o_ref[...] = jnp.zeros(o_ref.shape, o_ref.dtype)


def _expert_ffn(xs, ends, tiles_e, w1, w3, w2, tm_e):
    r, d = xs.shape
    _, _, f = w1.shape
    n_tiles = r // tm_e
    tf = _tile(f, 512)
    tn = _tile(d, 1024)
    row_map = lambda s, tile, col, *_: (tile[s], 0)
    out_map = lambda s, tile, col, *_: (tile[s], col[s])

    a_s = pl.pallas_call(
        functools.partial(_gm1_kernel, tf=tf),
        out_shape=jax.ShapeDtypeStruct((r, f), BF16),
        grid_spec=pltpu.PrefetchScalarGridSpec(
            num_scalar_prefetch=6,
            grid=(n_tiles * (f // tf),),
            in_specs=[pl.BlockSpec((tm_e, d), row_map),
                      pl.BlockSpec(memory_space=pl.ANY), pl.BlockSpec(memory_space=pl.ANY)],
            out_specs=pl.BlockSpec((tm_e, tf), out_map),
            scratch_shapes=[pltpu.VMEM((2, d, tf), F32), pltpu.VMEM((2, d, tf), BF16),
                            pltpu.SemaphoreType.DMA((2,))],
        ),
        compiler_params=_params(1),
        name="moe_expert_up",
    )(*_work_items(ends, tiles_e, n_tiles, f // tf), xs, w1, w3)

    return pl.pallas_call(
        functools.partial(_gm2_kernel, tn=tn),
        out_shape=jax.ShapeDtypeStruct((r, d), F32),
        grid_spec=pltpu.PrefetchScalarGridSpec(
            num_scalar_prefetch=6,
            grid=(n_tiles * (d // tn),),
            in_specs=[pl.BlockSpec((tm_e, f), row_map),
                      pl.BlockSpec(memory_space=pl.ANY)],
            out_specs=pl.BlockSpec((tm_e, tn), out_map),
            scratch_shapes=[pltpu.VMEM((1, f, tn), F32), pltpu.VMEM((1, f, tn), BF16),
                            pltpu.SemaphoreType.DMA((1,))],
        ),
        compiler_params=_params(1),
        name="moe_expert_down",
    )(*_work_items(ends, tiles_e, n_tiles, d // tn), a_s, w2)


def _combine_kernel(pos_ref, ys_hbm, x_ref, meta_ref, mod_ref, g_ref, o_ref, buf, sem, *, tm, n_steps):
    i = pl.program_id(0)

    def row_copy(half, r, k, src_row):
        return pltpu.make_async_copy(ys_hbm.at[pl.ds(src_row, 1), :], buf.at[half, k, pl.ds(r, 1), :],
                                     sem.at[half, k])

    def issue(step):
        half = step % 2
        base = step * tm

        def body(r, carry):
            for k in range(2):
                row_copy(half, r, k, pos_ref[2 * (base + r) + k]).start()
            return carry
        lax.fori_loop(0, tm, body, 0)

    @pl.when(i == 0)
    def _():
        issue(i)

    @pl.when(i + 1 < n_steps)
    def _():
        issue(i + 1)

    half = i % 2

    def wait(r, carry):
        for k in range(2):
            row_copy(half, r, k, 0).wait()
        return carry
    lax.fori_loop(0, tm, wait, 0)

    y = meta_ref[:, 4:5] * buf[half, 0] + meta_ref[:, 5:6] * buf[half, 1]
    xn = x_ref[...] + mod_ref[2:3, :] * y
    var = jnp.mean(xn * xn, axis=-1, keepdims=True)
    o_ref[...] = (xn * lax.rsqrt(var + EPS)) * g_ref[...]


def _combine_final(ys, pos, x, meta, mods, li, seq, g_final):
    t, d = x.shape
    tm = _tile(seq, 128, 8)
    return pl.pallas_call(
        functools.partial(_combine_kernel, tm=tm, n_steps=t // tm),
        out_shape=jax.ShapeDtypeStruct((t, d), F32),
        grid_spec=pltpu.PrefetchScalarGridSpec(
            num_scalar_prefetch=1,
            grid=(t // tm,),
            in_specs=[pl.BlockSpec(memory_space=pl.ANY),
                      pl.BlockSpec((tm, d), lambda m, pos_ref: (m, 0)),
                      pl.BlockSpec((tm, LANE), lambda m, pos_ref: (m, 0)),
                      pl.BlockSpec((None, None, 3, d), lambda m, pos_ref: (li, m // (seq // tm), 0, 0)),
                      pl.BlockSpec((1, d), lambda m, pos_ref: (0, 0))],
            out_specs=pl.BlockSpec((tm, d), lambda m, pos_ref: (m, 0)),
            scratch_shapes=[pltpu.VMEM((2, 2, tm, d), F32), pltpu.SemaphoreType.DMA((2, 2))],
        ),
        compiler_params=_params(1),
        name="moe_combine_final_norm",
    )(pos, ys, x, meta, mods, g_final.reshape(1, d))


def _rot_half_cols(w):
    half = w.shape[-1] // 2
    return jnp.concatenate([-w[..., half:], w[..., :half]], axis=-1)


def _prep_w_in(w_in, q_lora, kv_lora, n_lat):
    d = w_in.shape[0]
    o1 = q_lora + kv_lora
    o2 = o1 + MLA_ROPE
    k_rope = w_in[:, o1:o2]
    lat = jnp.concatenate([w_in[:, :o1], k_rope, _rot_half_cols(k_rope)], axis=1).astype(BF16)
    lat = jnp.pad(lat, ((0, 0), (0, n_lat - lat.shape[1])))
    return lat, w_in[:, o2:].astype(BF16)


def _prep_w_uq(w_uq, heads):
    ql = w_uq.shape[0]
    w = w_uq.reshape(ql, heads, MLA_NOPE + MLA_ROPE)
    rope = w[..., MLA_NOPE:]
    return jnp.concatenate([w[..., :MLA_NOPE], rope, _rot_half_cols(rope)], axis=-1
                           ).reshape(ql, heads * QK_HEAD).astype(BF16)


def _prep_w_ukv(w_ukv, heads):
    kvl = w_ukv.shape[0]
    w = w_ukv.reshape(kvl, heads, MLA_NOPE + MLA_V)
    return jnp.concatenate([w[..., :MLA_NOPE].reshape(kvl, heads * MLA_NOPE),
                            w[..., MLA_NOPE:].reshape(kvl, heads * MLA_V)], axis=1).astype(BF16)


def kernel(x, c, positions, ada_w, ada_b, norm_g, even_w_in, even_q_norm_g, even_kv_norm_g, even_w_uq, even_w_ukv, even_conv_w, even_w_out, even_ffn_w1, even_ffn_w3, even_ffn_w2, odd_w_in, odd_conv_w, odd_conv_b, odd_gate_a_w, odd_gate_a_b, odd_gate_x_w, odd_gate_x_b, odd_lambda, odd_w_out, odd_router_w, odd_router_b, odd_exp_w1, odd_exp_w3, odd_exp_w2, final_norm_g):
    batch, seq, d = x.shape
    t = batch * seq
    q_lora = even_q_norm_g.shape[1]
    kv_lora = even_kv_norm_g.shape[1]
    heads = even_w_uq.shape[2] // (MLA_NOPE + MLA_ROPE)
    conv_dim = even_conv_w.shape[2]
    d_ff = even_ffn_w1.shape[2]
    n_exp = odd_router_w.shape[2]
    assert even_w_in.shape[0] == 1 and odd_w_in.shape[0] == 1, "one layer of each type"

    xf = x.reshape(t, d)
    mods = _ada_modulation(c, ada_w, ada_b)
    rt = _rope_table(positions)
    tm = _tile(seq, 1024, 16)
    tm_dual = _tile(seq, 512, 16)

    n_lat = _round_up(q_lora + kv_lora + LANE, 256)
    w_lat, w_cv = _prep_w_in(even_w_in[0], q_lora, kv_lora, n_lat)
    h = _norm_mod(xf, norm_g[0], mods, 0, seq)
    lat = _matmul([h], [w_lat], [0], _epi_store, [], [], [jax.ShapeDtypeStruct((t, n_lat), BF16)],
                  tm=tm, tn=_tile(n_lat, 1024), name="in_proj_0_latents")[0]
    cv = _matmul([h], [w_cv], [0], _epi_store, [], [], [jax.ShapeDtypeStruct((t, 3 * conv_dim), BF16)],
                 tm=tm, tn=_tile(3 * conv_dim, 1024), name="in_proj_0_conv")[0]
    q = _q_projection(lat, even_q_norm_g[0], rt, _prep_w_uq(even_w_uq[0], heads), heads, q_lora)
    kn, v, kr = _kv_projection(lat, even_kv_norm_g[0], rt, _prep_w_ukv(even_w_ukv[0], heads),
                               heads, q_lora, kv_lora)
    attn = _attention(q, kn, kr, v, batch, seq, heads)
    conv = _gated_conv(cv, even_conv_w[0], seq, conv_dim)
    xf = _ws_residual_matmul([attn, conv], even_w_out, xf, mods, 0, seq, tm=tm, tn=_tile(d, 512),
                             name="out_proj_0")

    f_pad = _round_up(d_ff, 512)
    w2 = jnp.pad(even_ffn_w2[0].astype(BF16), ((0, f_pad - d_ff), (0, 0)))
    h = _norm_mod(xf, norm_g[1], mods, 1, seq)
    act = _ws_matmul([h], [even_ffn_w1, even_ffn_w3], [0, 0], d_ff, _epi_swiglu, [], [],
                     [jax.ShapeDtypeStruct((t, f_pad), BF16)], tm=tm_dual, tn=_tile(f_pad, 512),
                     name="ffn_up")[0]
    xf = _residual_matmul([act], w2, xf, mods, 1, seq, tm=tm, tn=_tile(d, 1024),
                          tk=_tile(f_pad, 3072), name="ffn_down")

    width = odd_conv_w.shape[2]
    h = _norm_mod(xf, norm_g[2], mods, 2, seq)
    gate_br, xb_pre = _ws_matmul([h], [odd_w_in, odd_w_in], [0, width], width, _epi_gelu_pair, [], [],
                                 [jax.ShapeDtypeStruct((t, width), BF16)] * 2, tm=tm_dual,
                                 tn=_tile(width, 512), name="in_proj_1")
    y = _rglru(xb_pre, gate_br, odd_conv_w[0], odd_conv_b[0], odd_gate_a_w[0].astype(BF16),
               odd_gate_a_b[0], odd_gate_x_w[0].astype(BF16), odd_gate_x_b[0], odd_lambda[0], batch, seq)
    xf = _ws_residual_matmul([y], odd_w_out, xf, mods, 2, seq, tm=tm, tn=_tile(d, 512), name="out_proj_1")

    tm_e = _tile(seq, 512, 16)
    n_rows = 2 * t + n_exp * tm_e
    n_tiles = n_rows // tm_e
    h32, meta, cnt = _router(xf, norm_g[3], mods, 3, seq, odd_router_w[0], odd_router_b[0])
    counts = cnt[0, :n_exp].astype(jnp.int32)
    tiles_e = (counts + tm_e - 1) // tm_e
    ends = jnp.cumsum(tiles_e)
    start_rows = (ends - tiles_e) * tm_e
    e_idx = meta[:, 0:2].astype(jnp.int32)
    pos = (jnp.take(start_rows, e_idx) + meta[:, 2:4].astype(jnp.int32)).reshape(2 * t)
    tid = jnp.arange(n_tiles, dtype=jnp.int32)
    group_last = jnp.any((tid[:, None] == ends[None, :] - 1) & (tiles_e[None, :] > 0), axis=1)
    zero_flag = (group_last | (tid >= ends[-1])).astype(jnp.int32)
    xs = _dispatch(h32, pos, zero_flag, n_rows, tm_e, seq)
    ys = _expert_ffn(xs, ends, tiles_e, odd_exp_w1[0], odd_exp_w3[0], odd_exp_w2[0], tm_e)
    out = _combine_final(ys, pos, xf, meta, mods, 3, seq, final_norm_g)
    return out.reshape(batch, seq, d)
```

```python
import functools

import jax
import jax.numpy as jnp
from jax import lax
from jax.experimental import pallas as pl
from jax.experimental.pallas import tpu as pltpu

F32 = jnp.float32
BF16 = jnp.bfloat16

EPS = 1e-6
ROPE_THETA = 10000.0
LRU_C = 8.0
MLA_NOPE = 128
MLA_ROPE = 64
MLA_V = 128
QK_HEAD = MLA_NOPE + 2 * MLA_ROPE
LANE = 128
ADA_ROWS = 16
VMEM_LIMIT_BYTES = 56 * 1024 * 1024


def _params(grid_rank):
    return pltpu.CompilerParams(dimension_semantics=("arbitrary",) * grid_rank,
                                vmem_limit_bytes=VMEM_LIMIT_BYTES)


def _tile(dim, pref, mult=LANE):
    if dim <= pref:
        return dim
    t = (pref // mult) * mult
    while t > mult and dim % t:
        t -= mult
    assert dim % t == 0, (dim, pref)
    return t


def _round_up(x, m):
    return (x + m - 1) // m * m


CAST_ROWS = 256


def _cast_rows(src_ref, dst_ref):
    rows = src_ref.shape[0]
    step = CAST_ROWS if rows % CAST_ROWS == 0 else rows
    for r in range(0, rows, step):
        dst_ref[r:r + step, :] = src_ref[r:r + step, :].astype(dst_ref.dtype)


def _ada_kernel(c_ref, w_ref, b_ref, o_ref):
    c = c_ref[...]
    sc = (c * jax.nn.sigmoid(c)).astype(BF16)
    o_ref[...] = jnp.dot(sc, w_ref[...].astype(BF16), preferred_element_type=F32) + b_ref[...]


def _ada_modulation(c, ada_w, ada_b):
    nmod, d, d3 = ada_w.shape
    b = c.shape[0]
    tn = _tile(d3, 512)
    cp = jnp.zeros((ADA_ROWS, d), F32).at[:b].set(c)
    out = pl.pallas_call(
        _ada_kernel,
        out_shape=jax.ShapeDtypeStruct((nmod, ADA_ROWS, d3), F32),
        grid=(nmod, d3 // tn),
        in_specs=[
            pl.BlockSpec((ADA_ROWS, d), lambda i, j: (0, 0)),
            pl.BlockSpec((None, d, tn), lambda i, j: (i, 0, j)),
            pl.BlockSpec((None, 1, tn), lambda i, j: (i, 0, j)),
        ],
        out_specs=pl.BlockSpec((None, ADA_ROWS, tn), lambda i, j: (i, 0, j)),
        compiler_params=_params(2),
        name="ada_modulation",
    )(cp, ada_w, ada_b.reshape(nmod, 1, d3))
    return out[:, :b].reshape(nmod, b, 3, d)


def _mod_spec(li, d_blk, rows_per_batch_tiles, col_map=None):
    if col_map is None:
        return pl.BlockSpec((None, None, 3, d_blk),
                            lambda m, *_: (li, m // rows_per_batch_tiles, 0, 0))
    return pl.BlockSpec((None, None, 3, d_blk),
                        lambda m, n, *_: (li, m // rows_per_batch_tiles, 0, n))


NORM_CHUNK = 16


def _norm_rows(x, gain, shift):
    var = jnp.mean(x * x, axis=-1, keepdims=True)
    return (x * lax.rsqrt(var + EPS)) * gain + shift


def _norm_kernel(x_ref, g_ref, mod_ref, h_ref, *, tm):
    gain = g_ref[...] * (1.0 + mod_ref[1:2, :])
    shift = mod_ref[0:1, :]

    def body(r, carry):
        r0 = pl.multiple_of(r * NORM_CHUNK, NORM_CHUNK)
        x = x_ref[pl.ds(r0, NORM_CHUNK), :]
        h_ref[pl.ds(r0, NORM_CHUNK), :] = _norm_rows(x, gain, shift).astype(h_ref.dtype)
        return carry
    lax.fori_loop(0, tm // NORM_CHUNK, body, 0, unroll=2)


def _norm_mod(x, g, mods, li, seq):
    t, d = x.shape
    tm = _tile(seq, 512, NORM_CHUNK)
    return pl.pallas_call(
        functools.partial(_norm_kernel, tm=tm),
        out_shape=jax.ShapeDtypeStruct((t, d), BF16),
        grid=(t // tm,),
        in_specs=[
            pl.BlockSpec((tm, d), lambda m: (m, 0)),
            pl.BlockSpec((1, d), lambda m: (0, 0)),
            _mod_spec(li, d, seq // tm),
        ],
        out_specs=pl.BlockSpec((tm, d), lambda m: (m, 0)),
        compiler_params=_params(1),
        name=f"norm_mod_{li}",
    )(x, g.reshape(1, d), mods)


def _mm_kernel(*refs, n_lhs, n_w, n_extra, n_out, nk, epilogue):
    lhs = refs[:n_lhs]
    ws = refs[n_lhs:n_lhs + n_w]
    extra = refs[n_lhs + n_w:n_lhs + n_w + n_extra]
    outs = refs[n_lhs + n_w + n_extra:n_lhs + n_w + n_extra + n_out]
    accs = refs[n_lhs + n_w + n_extra + n_out:]

    def partial_product(w_ref):
        off, tot = 0, None
        for l_ref in lhs:
            kk = l_ref.shape[1]
            part = jnp.dot(l_ref[...], w_ref[off:off + kk, :], preferred_element_type=F32)
            tot = part if tot is None else tot + part
            off += kk
        return tot

    if nk == 1:
        _run_epilogue(epilogue, [partial_product(w) for w in ws], extra, outs)
        return

    k = pl.program_id(2)

    @pl.when(k == 0)
    def _():
        for acc, w in zip(accs, ws):
            acc[...] = partial_product(w)

    @pl.when(k > 0)
    def _():
        for acc, w in zip(accs, ws):
            acc[...] += partial_product(w)

    @pl.when(k == nk - 1)
    def _():
        _run_epilogue(epilogue, [acc[...] for acc in accs], extra, outs)


def _matmul(lhs_list, w_list, w_col_offsets, epilogue, extras, extra_specs, out_shapes,
            *, tm, tn, tk=None, name):
    t = lhs_list[0].shape[0]
    ktot = sum(l.shape[1] for l in lhs_list)
    n = out_shapes[0].shape[1]
    if tk is None:
        nk = 1
        lhs_specs = [pl.BlockSpec((tm, l.shape[1]), lambda m, j, k: (m, 0)) for l in lhs_list]
        w_rows = ktot
    else:
        assert len(lhs_list) == 1 and ktot % tk == 0
        nk = ktot // tk
        lhs_specs = [pl.BlockSpec((tm, tk), lambda m, j, k: (m, k))]
        w_rows = tk
    w_specs = [pl.BlockSpec((w_rows, tn), functools.partial(lambda m, j, k, off: (k, j + off), off=off))
               for off in w_col_offsets]
    kern = functools.partial(_mm_kernel, n_lhs=len(lhs_list), n_w=len(w_list), n_extra=len(extras),
                             n_out=len(out_shapes), nk=nk, epilogue=epilogue)
    scratch = [pltpu.VMEM((tm, tn), F32) for _ in w_list] if nk > 1 else []
    return pl.pallas_call(
        kern,
        out_shape=out_shapes,
        grid=(t // tm, n // tn, nk),
        in_specs=lhs_specs + w_specs + list(extra_specs),
        out_specs=[pl.BlockSpec((tm, tn), lambda m, j, k: (m, j)) for _ in out_shapes],
        scratch_shapes=scratch,
        compiler_params=_params(3),
        name=name,
    )(*lhs_list, *w_list, *extras)


def _ws_kernel(*refs, n_lhs, n_w, n_extra, n_out, tn, nj, col_offsets, rem, epilogue):
    lhs = refs[:n_lhs]
    w_hbm = refs[n_lhs:n_lhs + n_w]
    extra = refs[n_lhs + n_w:n_lhs + n_w + n_extra]
    outs = refs[n_lhs + n_w + n_extra:n_lhs + n_w + n_extra + n_out]
    stage, wb, sem = refs[n_lhs + n_w + n_extra + n_out:]
    j = pl.program_id(0)
    m = pl.program_id(1)

    def wcopy(col_tile, k, width):
        cols = pl.ds(pl.multiple_of(col_tile * tn + col_offsets[k], LANE), width)
        dst = stage.at[k] if width == tn else stage.at[k, :, 0:width]
        return pltpu.make_async_copy(w_hbm[k].at[0, :, cols], dst, sem.at[k])

    def for_tile(col_tile, fn):
        if rem == tn:
            for k in range(n_w):
                fn(wcopy(col_tile, k, tn))
            return

        @pl.when(col_tile < nj - 1)
        def _():
            for k in range(n_w):
                fn(wcopy(col_tile, k, tn))

        @pl.when(col_tile == nj - 1)
        def _():
            for k in range(n_w):
                fn(wcopy(col_tile, k, rem))

    @pl.when((j == 0) & (m == 0))
    def _():
        for_tile(j, lambda c: c.start())

    @pl.when(m == 0)
    def _():
        for_tile(j, lambda c: c.wait())
        for k in range(n_w):
            _cast_rows(stage.at[k], wb.at[k])
        if rem != tn:
            @pl.when(j == nj - 1)
            def _():
                for k in range(n_w):
                    wb[k, :, rem:] = jnp.zeros((wb.shape[1], tn - rem), BF16)

        @pl.when(j + 1 < nj)
        def _():
            for_tile(j + 1, lambda c: c.start())

    accs = []
    for k in range(n_w):
        off, tot = 0, None
        for l_ref in lhs:
            kk = l_ref.shape[1]
            part = jnp.dot(l_ref[...], wb[k, off:off + kk, :], preferred_element_type=F32)
            tot = part if tot is None else tot + part
            off += kk
        accs.append(tot)
    _run_epilogue(epilogue, accs, extra, outs)


def _ws_matmul(lhs_list, w_list, col_offsets, n_valid, epilogue, extras, extra_specs, out_shapes,
               *, tm, tn, name):
    t = lhs_list[0].shape[0]
    ktot = sum(l.shape[1] for l in lhs_list)
    n_out = out_shapes[0].shape[1]
    nj = n_out // tn
    rem = n_valid - (nj - 1) * tn
    assert 0 < rem <= tn and (rem == tn or nj > 1)
    kern = functools.partial(_ws_kernel, n_lhs=len(lhs_list), n_w=len(w_list), n_extra=len(extras),
                             n_out=len(out_shapes), tn=tn, nj=nj, col_offsets=tuple(col_offsets), rem=rem,
                             epilogue=epilogue)
    return pl.pallas_call(
        kern,
        out_shape=out_shapes,
        grid=(nj, t // tm),
        in_specs=[pl.BlockSpec((tm, l.shape[1]), lambda j, m: (m, 0)) for l in lhs_list]
                 + [pl.BlockSpec(memory_space=pl.ANY) for _ in w_list] + list(extra_specs),
        out_specs=[pl.BlockSpec((tm, tn), lambda j, m: (m, j)) for _ in out_shapes],
        scratch_shapes=[pltpu.VMEM((len(w_list), ktot, tn), F32), pltpu.VMEM((len(w_list), ktot, tn), BF16),
                        pltpu.SemaphoreType.DMA((len(w_list),))],
        compiler_params=_params(2),
        name=name,
    )(*lhs_list, *w_list, *extras)


def _ws_residual_matmul(lhs_list, w, x, mods, li, seq, *, tm, tn, name):
    t, d = x.shape
    tiles_per_batch = seq // tm
    extras = [x, mods]
    extra_specs = [pl.BlockSpec((tm, tn), lambda j, m: (m, j)),
                   pl.BlockSpec((None, None, 3, tn), lambda j, m: (li, m // tiles_per_batch, 0, j))]
    return _ws_matmul(lhs_list, [w], [0], d, _epi_residual, extras, extra_specs,
                      [jax.ShapeDtypeStruct((t, d), F32)], tm=tm, tn=tn, name=name)[0]


EPILOGUE_ROWS = 128


def _run_epilogue(epilogue, accs, extra, outs):
    rows = accs[0].shape[0]
    step = EPILOGUE_ROWS if rows % EPILOGUE_ROWS == 0 else rows
    for r in range(0, rows, step):
        epilogue([a[r:r + step] for a in accs], extra, outs, slice(r, r + step))


def _epi_store(accs, extra, outs, rs):
    outs[0][rs, :] = accs[0].astype(outs[0].dtype)


def _epi_residual(accs, extra, outs, rs):
    x_ref, mod_ref = extra
    outs[0][rs, :] = x_ref[rs, :] + mod_ref[2:3, :] * accs[0]


def _epi_swiglu(accs, extra, outs, rs):
    a, b = accs
    outs[0][rs, :] = (jax.nn.silu(a) * b).astype(outs[0].dtype)


def _epi_gelu_pair(accs, extra, outs, rs):
    a, b = accs
    outs[0][rs, :] = jax.nn.gelu(a, approximate=True).astype(outs[0].dtype)
    outs[1][rs, :] = b.astype(outs[1].dtype)


def _residual_matmul(lhs_list, w, x, mods, li, seq, *, tm, tn, tk=None, name):
    t, d = x.shape
    extras = [x, mods]
    extra_specs = [pl.BlockSpec((tm, tn), lambda m, j, k: (m, j)),
                   _mod_spec(li, tn, seq // tm, col_map=True)]
    return _matmul(lhs_list, [w], [0], _epi_residual, extras, extra_specs,
                   [jax.ShapeDtypeStruct((t, d), F32)], tm=tm, tn=tn, tk=tk, name=name)[0]


def _rope_kernel(pos_ref, inv_ref, o_ref):
    ang = pos_ref[...].astype(F32) * inv_ref[...]
    lane = lax.broadcasted_iota(jnp.int32, ang.shape, 1)
    o_ref[...] = jnp.where(lane < MLA_ROPE, jnp.cos(ang), jnp.sin(ang))


def _rope_table(positions):
    t = positions.size
    tm = _tile(t, 1024, 8)
    half = MLA_ROPE // 2
    inv = 1.0 / (ROPE_THETA ** (jnp.arange(0, MLA_ROPE, 2, dtype=F32) / MLA_ROPE))
    inv4 = jnp.tile(inv, 4).reshape(1, 4 * half)
    return pl.pallas_call(
        _rope_kernel,
        out_shape=jax.ShapeDtypeStruct((t, LANE), F32),
        grid=(t // tm,),
        in_specs=[pl.BlockSpec((tm, 1), lambda m: (m, 0)),
                  pl.BlockSpec((1, LANE), lambda m: (0, 0))],
        out_specs=pl.BlockSpec((tm, LANE), lambda m: (m, 0)),
        compiler_params=_params(1),
        name="rope_table",
    )(positions.reshape(t, 1), inv4)


def _qproj_kernel(cq_ref, g_ref, rt_ref, w_ref, o_ref, nq_ref, *, heads_per_tile, scale):
    @pl.when(pl.program_id(1) == 0)
    def _():
        cq = cq_ref[...].astype(F32)
        var = jnp.mean(cq * cq, axis=-1, keepdims=True)
        nq_ref[...] = ((cq * lax.rsqrt(var + EPS)) * g_ref[...]).astype(BF16)

    res = jnp.dot(nq_ref[...], w_ref[...], preferred_element_type=F32)
    rt = rt_ref[...] * scale
    for hh in range(heads_per_tile):
        c0 = hh * QK_HEAD
        o_ref[:, c0:c0 + MLA_NOPE] = (res[:, c0:c0 + MLA_NOPE] * scale).astype(o_ref.dtype)
        o_ref[:, c0 + MLA_NOPE:c0 + QK_HEAD] = (res[:, c0 + MLA_NOPE:c0 + QK_HEAD] * rt).astype(o_ref.dtype)


def _q_projection(proj, g_q, rt, w_uq_p, heads, q_lora):
    t = proj.shape[0]
    tm = _tile(t, 1024, 16)
    hpt = min(heads, 4)
    tn = hpt * QK_HEAD
    scale = (MLA_NOPE + MLA_ROPE) ** -0.5
    return pl.pallas_call(
        functools.partial(_qproj_kernel, heads_per_tile=hpt, scale=scale),
        out_shape=jax.ShapeDtypeStruct((t, heads * QK_HEAD), BF16),
        grid=(t // tm, heads // hpt),
        in_specs=[
            pl.BlockSpec((tm, q_lora), lambda m, j: (m, 0)),
            pl.BlockSpec((1, q_lora), lambda m, j: (0, 0)),
            pl.BlockSpec((tm, LANE), lambda m, j: (m, 0)),
            pl.BlockSpec((q_lora, tn), lambda m, j: (0, j)),
        ],
        out_specs=pl.BlockSpec((tm, tn), lambda m, j: (m, j)),
        scratch_shapes=[pltpu.VMEM((tm, q_lora), BF16)],
        compiler_params=_params(2),
        name="q_projection",
    )(proj, g_q.reshape(1, q_lora), rt, w_uq_p)


def _kvproj_kernel(ckv_ref, kr_ref, g_ref, rt_ref, w_ref, kn_ref, v_ref, kro_ref, *, hv):
    ckv = ckv_ref[...].astype(F32)
    var = jnp.mean(ckv * ckv, axis=-1, keepdims=True)
    nkv = ((ckv * lax.rsqrt(var + EPS)) * g_ref[...]).astype(BF16)
    res = jnp.dot(nkv, w_ref[...], preferred_element_type=F32)
    kn_ref[...] = res[:, :hv].astype(kn_ref.dtype)
    v_ref[...] = res[:, hv:].astype(v_ref.dtype)
    kv = kr_ref[...].astype(F32) * rt_ref[...]
    kro_ref[...] = (kv + pltpu.roll(kv, MLA_ROPE, 1)).astype(kro_ref.dtype)


def _kv_projection(proj, g_kv, rt, w_ukv_p, heads, q_lora, kv_lora):
    t = proj.shape[0]
    tm = _tile(t, 512, 16)
    hv = heads * MLA_NOPE
    return pl.pallas_call(
        functools.partial(_kvproj_kernel, hv=hv),
        out_shape=[jax.ShapeDtypeStruct((t, hv), BF16), jax.ShapeDtypeStruct((t, hv), BF16),
                   jax.ShapeDtypeStruct((t, LANE), BF16)],
        grid=(t // tm,),
        in_specs=[
            pl.BlockSpec((tm, kv_lora), lambda m: (m, q_lora // kv_lora)),
            pl.BlockSpec((tm, LANE), lambda m: (m, (q_lora + kv_lora) // LANE)),
            pl.BlockSpec((1, kv_lora), lambda m: (0, 0)),
            pl.BlockSpec((tm, LANE), lambda m: (m, 0)),
            pl.BlockSpec((kv_lora, 2 * hv), lambda m: (0, 0)),
        ],
        out_specs=[pl.BlockSpec((tm, hv), lambda m: (m, 0)), pl.BlockSpec((tm, hv), lambda m: (m, 0)),
                   pl.BlockSpec((tm, LANE), lambda m: (m, 0))],
        compiler_params=_params(1),
        name="kv_projection",
    )(proj, proj, g_kv.reshape(1, kv_lora), rt, w_ukv_p)


def _attn_kernel(q_ref, kn_ref, kr_ref, v_ref, o_ref, *, tq, hpb):
    qi = pl.program_id(2)
    ones = jnp.ones((tq, MLA_V), BF16)

    def block(ki, carry, masked):
        ks = pl.multiple_of(ki * tq, tq)
        kr = kr_ref[pl.ds(ks, tq), :]
        new = []
        for hh in range(hpb):
            m, acc = carry[hh]
            q = q_ref[:, hh * QK_HEAD:(hh + 1) * QK_HEAD]
            k = jnp.concatenate([kn_ref[pl.ds(ks, tq), hh * MLA_NOPE:(hh + 1) * MLA_NOPE], kr], axis=1)
            s = lax.dot_general(q, k, (((1,), (1,)), ((), ())), preferred_element_type=F32)
            if masked:
                row = lax.broadcasted_iota(jnp.int32, s.shape, 0)
                col = lax.broadcasted_iota(jnp.int32, s.shape, 1)
                s = jnp.where(col <= row, s, -jnp.inf)
            m_new = jnp.maximum(m, jnp.max(s, axis=1, keepdims=True))
            alpha = jnp.exp(m - m_new)
            p = jnp.exp(s - m_new).astype(BF16)
            v1 = jnp.concatenate([v_ref[pl.ds(ks, tq), hh * MLA_V:(hh + 1) * MLA_V], ones], axis=1)
            acc = alpha * acc + jnp.dot(p, v1, preferred_element_type=F32)
            new.append((m_new, acc))
        return tuple(new)

    init = tuple((jnp.full((tq, 1), -jnp.inf, F32), jnp.zeros((tq, 2 * MLA_V), F32)) for _ in range(hpb))
    carry = lax.fori_loop(0, qi, lambda ki, c: block(ki, c, False), init)
    carry = block(qi, carry, True)
    for hh in range(hpb):
        acc = carry[hh][1]
        o_ref[:, hh * MLA_V:(hh + 1) * MLA_V] = (acc[:, :MLA_V] / acc[:, MLA_V:]).astype(o_ref.dtype)


def _attention(q, kn, kr, v, batch, seq, heads):
    t = q.shape[0]
    tq = _tile(seq, 512, 16)
    nq = seq // tq
    hpb = 2 if heads % 2 == 0 else 1
    return pl.pallas_call(
        functools.partial(_attn_kernel, tq=tq, hpb=hpb),
        out_shape=jax.ShapeDtypeStruct((t, heads * MLA_V), BF16),
        grid=(batch, heads // hpb, nq),
        in_specs=[
            pl.BlockSpec((tq, hpb * QK_HEAD), lambda b, h, i: (b * nq + i, h)),
            pl.BlockSpec((seq, hpb * MLA_NOPE), lambda b, h, i: (b, h)),
            pl.BlockSpec((seq, LANE), lambda b, h, i: (b, 0)),
            pl.BlockSpec((seq, hpb * MLA_V), lambda b, h, i: (b, h)),
        ],
        out_specs=pl.BlockSpec((tq, hpb * MLA_V), lambda b, h, i: (b * nq + i, h)),
        compiler_params=_params(3),
        name="mla_attention",
    )(q, kn, kr, v)


HALO = 16


def _conv3_kernel(gb_ref, gc_ref, u_ref, gch_ref, uh_ref, w_ref, o_ref, ext_ref, *, tm, tiles_per_seq, taps):
    first = (pl.program_id(0) % tiles_per_seq) == 0
    halo = gch_ref[...].astype(F32) * uh_ref[...].astype(F32)
    ext_ref[0:HALO, :] = jnp.where(first, 0.0, halo)
    ext_ref[HALO:, :] = gc_ref[...].astype(F32) * u_ref[...].astype(F32)
    y = None
    for j in range(taps):
        off = HALO - (taps - 1 - j)
        term = w_ref[j:j + 1, :] * ext_ref[off:off + tm, :]
        y = term if y is None else y + term
    o_ref[...] = (gb_ref[...].astype(F32) * y).astype(o_ref.dtype)


def _gated_conv(proj, conv_w, seq, conv_dim):
    t = proj.shape[0]
    taps = conv_w.shape[0]
    tm = _tile(seq, 512, HALO)
    tc = _tile(conv_dim, 512)
    ob, oc, ou = 0, conv_dim // tc, 2 * conv_dim // tc
    hb = tm // HALO

    def halo_map(off):
        return lambda m, c: (jnp.maximum(m * hb - 1, 0), off + c)

    return pl.pallas_call(
        functools.partial(_conv3_kernel, tm=tm, tiles_per_seq=seq // tm, taps=taps),
        out_shape=jax.ShapeDtypeStruct((t, conv_dim), BF16),
        grid=(t // tm, conv_dim // tc),
        in_specs=[
            pl.BlockSpec((tm, tc), lambda m, c: (m, ob + c)),
            pl.BlockSpec((tm, tc), lambda m, c: (m, oc + c)),
            pl.BlockSpec((tm, tc), lambda m, c: (m, ou + c)),
            pl.BlockSpec((HALO, tc), halo_map(oc)),
            pl.BlockSpec((HALO, tc), halo_map(ou)),
            pl.BlockSpec((taps, tc), lambda m, c: (0, c)),
        ],
        out_specs=pl.BlockSpec((tm, tc), lambda m, c: (m, c)),
        scratch_shapes=[pltpu.VMEM((tm + HALO, tc), F32)],
        compiler_params=_params(2),
        name="gated_conv3",
    )(proj, proj, proj, proj, proj, conv_w)


GROUP = 8


def _lru_kernel(xb_ref, gbr_ref, cw_ref, cb_ref, wa_ref, ba_ref, wx_ref, bx_ref, lam_ref, o_ref,
                ext_ref, a_ref, b_ref, hc_ref, *, tm, hd, hp, taps):
    @pl.when(pl.program_id(2) == 0)
    def _():
        ext_ref[0:GROUP, :] = jnp.zeros((GROUP, ext_ref.shape[1]), F32)
        hc_ref[...] = jnp.zeros(hc_ref.shape, F32)

    ext_ref[GROUP:, :] = xb_ref[...].astype(F32)
    xb = cb_ref[...]
    for j in range(taps):
        off = GROUP - (taps - 1 - j)
        xb = xb + cw_ref[j:j + 1, :] * ext_ref[off:off + tm, :]
    ext_ref[0:GROUP, :] = ext_ref[tm:tm + GROUP, :]

    xb16 = xb.astype(BF16)
    rs, is_ = [], []
    for h in range(hp):
        xh = xb16[:, h * hd:(h + 1) * hd]
        rs.append(jnp.dot(xh, wa_ref[h], preferred_element_type=F32))
        is_.append(jnp.dot(xh, wx_ref[h], preferred_element_type=F32))
    r = jax.nn.sigmoid(jnp.concatenate(rs, axis=1) + ba_ref[...])
    ig = jax.nn.sigmoid(jnp.concatenate(is_, axis=1) + bx_ref[...])
    lam = lam_ref[...]
    log_sig = -(jnp.maximum(-lam, 0.0) + jnp.log(1.0 + jnp.exp(-jnp.abs(lam))))
    log_a = (LRU_C * r) * log_sig
    a = jnp.exp(log_a)
    th = jnp.tanh(log_a)
    mult = jnp.sqrt(-2.0 * th / (1.0 - th))
    a_ref[...] = a
    b_ref[...] = mult * (ig * xb)

    rowg = lax.broadcasted_iota(jnp.int32, (GROUP, a_ref.shape[1]), 0)

    def group(gi, hc):
        r0 = pl.multiple_of(gi * GROUP, GROUP)
        ag = a_ref[pl.ds(r0, GROUP), :]
        bg = b_ref[pl.ds(r0, GROUP), :]
        for dist in (1, 2, 4):
            keep = rowg >= dist
            ap = jnp.where(keep, pltpu.roll(ag, dist, 0), 1.0)
            bp = jnp.where(keep, pltpu.roll(bg, dist, 0), 0.0)
            bg = ag * bp + bg
            ag = ag * ap
        hs = ag * hc + bg
        y = hs * gbr_ref[pl.ds(r0, GROUP), :].astype(F32)
        b_ref[pl.ds(r0, GROUP), :] = y
        return jnp.broadcast_to(hs[GROUP - 1:GROUP, :], hs.shape)

    hc_ref[...] = lax.fori_loop(0, tm // GROUP, group, hc_ref[...], unroll=4)
    o_ref[...] = b_ref[...].astype(o_ref.dtype)


def _rglru(xb_pre, gate_br, conv_w, conv_b, wa, ba, wx, bx, lam, batch, seq):
    t, width = xb_pre.shape
    heads, hd, _ = wa.shape
    taps = conv_w.shape[0]
    hp = 2 if heads % 2 == 0 else 1
    c = hp * hd
    tm = _tile(seq, 512, 16)
    nt = seq // tm
    row = lambda b, h, i: (b * nt + i, h)
    vec = lambda b, h, i: (0, h)
    return pl.pallas_call(
        functools.partial(_lru_kernel, tm=tm, hd=hd, hp=hp, taps=taps),
        out_shape=jax.ShapeDtypeStruct((t, width), BF16),
        grid=(batch, heads // hp, nt),
        in_specs=[
            pl.BlockSpec((tm, c), row),
            pl.BlockSpec((tm, c), row),
            pl.BlockSpec((taps, c), vec),
            pl.BlockSpec((1, c), vec),
            pl.BlockSpec((hp, hd, hd), lambda b, h, i: (h, 0, 0)),
            pl.BlockSpec((1, c), vec),
            pl.BlockSpec((hp, hd, hd), lambda b, h, i: (h, 0, 0)),
            pl.BlockSpec((1, c), vec),
            pl.BlockSpec((1, c), vec),
        ],
        out_specs=pl.BlockSpec((tm, c), row),
        scratch_shapes=[pltpu.VMEM((tm + GROUP, c), F32), pltpu.VMEM((tm, c), F32),
                        pltpu.VMEM((tm, c), F32), pltpu.VMEM((GROUP, c), F32)],
        compiler_params=_params(3),
        name="rglru",
    )(xb_pre, gate_br, conv_w, conv_b.reshape(1, width), wa, ba.reshape(1, width), wx,
      bx.reshape(1, width), lam.reshape(1, width))


def _router_kernel(x_ref, g_ref, mod_ref, rw_ref, rb_ref, h_ref, meta_ref, cnt_ref, run_ref, hi_ref, lo_ref,
                   *, tm, n_exp):
    @pl.when(pl.program_id(0) == 0)
    def _():
        run_ref[...] = jnp.zeros(run_ref.shape, F32)

    gain = g_ref[...] * (1.0 + mod_ref[1:2, :])
    shift = mod_ref[0:1, :]

    def body(r, carry):
        r0 = pl.multiple_of(r * NORM_CHUNK, NORM_CHUNK)
        h = _norm_rows(x_ref[pl.ds(r0, NORM_CHUNK), :], gain, shift)
        hi = h.astype(BF16)
        h_ref[pl.ds(r0, NORM_CHUNK), :] = h
        hi_ref[pl.ds(r0, NORM_CHUNK), :] = hi
        lo_ref[pl.ds(r0, NORM_CHUNK), :] = (h - hi.astype(F32)).astype(BF16)
        return carry
    lax.fori_loop(0, tm // NORM_CHUNK, body, 0, unroll=2)

    logits = (jnp.dot(hi_ref[...], rw_ref[0], preferred_element_type=F32)
              + jnp.dot(lo_ref[...], rw_ref[0], preferred_element_type=F32)
              + jnp.dot(hi_ref[...], rw_ref[1], preferred_element_type=F32)) + rb_ref[...]
    lane = lax.broadcasted_iota(jnp.int32, logits.shape, 1)
    lg = jnp.where(lane < n_exp, logits, -jnp.inf)
    m1 = jnp.max(lg, axis=1, keepdims=True)
    i1 = jnp.min(jnp.where(lg == m1, lane, LANE), axis=1, keepdims=True)
    lg2 = jnp.where(lane == i1, -jnp.inf, lg)
    m2 = jnp.max(lg2, axis=1, keepdims=True)
    i2 = jnp.min(jnp.where(lg2 == m2, lane, LANE), axis=1, keepdims=True)
    e2 = jnp.exp(m2 - m1)
    den = 1.0 + e2
    w1 = 1.0 / den
    w2 = e2 / den

    hit1 = lane == i1
    hit2 = lane == i2
    sel = jnp.where(hit1 | hit2, 1.0, 0.0)
    rowi = lax.broadcasted_iota(jnp.int32, (tm, tm), 0)
    coli = lax.broadcasted_iota(jnp.int32, (tm, tm), 1)
    earlier = jnp.where(coli < rowi, 1.0, 0.0).astype(BF16)
    rank = jnp.dot(earlier, sel.astype(BF16), preferred_element_type=F32) + run_ref[0:1, :]
    r1 = jnp.sum(jnp.where(hit1, rank, 0.0), axis=1, keepdims=True)
    r2 = jnp.sum(jnp.where(hit2, rank, 0.0), axis=1, keepdims=True)
    run_ref[...] = run_ref[...] + jnp.sum(sel, axis=0, keepdims=True)
    cnt_ref[...] = run_ref[...]

    meta = jnp.where(lane == 0, i1.astype(F32), 0.0)
    meta = jnp.where(lane == 1, i2.astype(F32), meta)
    meta = jnp.where(lane == 2, r1, meta)
    meta = jnp.where(lane == 3, r2, meta)
    meta = jnp.where(lane == 4, w1, meta)
    meta = jnp.where(lane == 5, w2, meta)
    meta_ref[...] = meta


def _router(x, g, mods, li, seq, router_w, router_b):
    t, d = x.shape
    n_exp = router_w.shape[1]
    tm = _tile(seq, 256, 16)
    rw = jnp.zeros((d, LANE), F32).at[:, :n_exp].set(router_w)
    rw_hi = rw.astype(BF16)
    rw = jnp.stack([rw_hi, (rw - rw_hi.astype(F32)).astype(BF16)])
    rb = jnp.zeros((1, LANE), F32).at[0, :n_exp].set(router_b)
    return pl.pallas_call(
        functools.partial(_router_kernel, tm=tm, n_exp=n_exp),
        out_shape=[jax.ShapeDtypeStruct((t, d), F32), jax.ShapeDtypeStruct((t, LANE), F32),
                   jax.ShapeDtypeStruct((GROUP, LANE), F32)],
        grid=(t // tm,),
        in_specs=[
            pl.BlockSpec((tm, d), lambda m: (m, 0)),
            pl.BlockSpec((1, d), lambda m: (0, 0)),
            _mod_spec(li, d, seq // tm),
            pl.BlockSpec((2, d, LANE), lambda m: (0, 0, 0)),
            pl.BlockSpec((1, LANE), lambda m: (0, 0)),
        ],
        out_specs=[pl.BlockSpec((tm, d), lambda m: (m, 0)), pl.BlockSpec((tm, LANE), lambda m: (m, 0)),
                   pl.BlockSpec((GROUP, LANE), lambda m: (0, 0))],
        scratch_shapes=[pltpu.VMEM((GROUP, LANE), F32), pltpu.VMEM((tm, d), BF16), pltpu.VMEM((tm, d), BF16)],
        compiler_params=_params(1),
        name="moe_router",
    )(x, g.reshape(1, d), mods, rw, rb)


def _dispatch_kernel(pos_ref, zf_ref, h_ref, xs_hbm, zbuf, zsem, sem, *, tm, n_tiles, zr):
    @pl.when(pl.program_id(0) == 0)
    def _():
        zbuf[...] = jnp.zeros(zbuf.shape, zbuf.dtype)

        def zero_copy(c):
            return pltpu.make_async_copy(zbuf, xs_hbm.at[pl.ds(c * zr, zr), :], zsem)

        def zstart(c, carry):
            @pl.when(zf_ref[c] > 0)
            def _():
                zero_copy(c).start()
            return carry
        lax.fori_loop(0, n_tiles, zstart, 0)

        def zwait(c, carry):
            @pl.when(zf_ref[c] > 0)
            def _():
                zero_copy(c).wait()
            return carry
        lax.fori_loop(0, n_tiles, zwait, 0)

    base = pl.program_id(0) * tm

    def row_copy(r, slot):
        return pltpu.make_async_copy(h_ref.at[pl.ds(r, 1), :], xs_hbm.at[pl.ds(slot, 1), :], sem)

    def issue(r, carry):
        row_copy(r, pos_ref[2 * (base + r)]).start()
        row_copy(r, pos_ref[2 * (base + r) + 1]).start()
        return carry
    lax.fori_loop(0, tm, issue, 0)

    def wait(r, carry):
        row_copy(r, 0).wait()
        row_copy(r, 0).wait()
        return carry
    lax.fori_loop(0, tm, wait, 0)


def _dispatch(h, pos, zero_flag, n_rows, zr, seq):
    t, d = h.shape
    tm = _tile(seq, 256, 8)
    n_tiles = n_rows // zr
    return pl.pallas_call(
        functools.partial(_dispatch_kernel, tm=tm, n_tiles=n_tiles, zr=zr),
        out_shape=jax.ShapeDtypeStruct((n_rows, d), h.dtype),
        grid_spec=pltpu.PrefetchScalarGridSpec(
            num_scalar_prefetch=2,
            grid=(t // tm,),
            in_specs=[pl.BlockSpec((tm, d), lambda m, pos_ref, zf_ref: (m, 0))],
            out_specs=pl.BlockSpec(memory_space=pl.ANY),
            scratch_shapes=[pltpu.VMEM((zr, d), h.dtype), pltpu.SemaphoreType.DMA(()),
                            pltpu.SemaphoreType.DMA(())],
        ),
        compiler_params=_params(1),
        name="moe_dispatch",
    )(pos, zero_flag, h)


ITEM_ZERO, ITEM_COMPUTE, ITEM_FIRST, ITEM_FIRST_MORE = 0, 1, 2, 3


def _work_items(ends, tiles_e, counts, tm_e, n_tiles, ncol):
    n_exp = ends.shape[0]
    s = jnp.arange(ncol * n_tiles, dtype=jnp.int32)
    n_valid = ncol * ends[-1]
    e = jnp.minimum(jnp.sum((s[:, None] >= ncol * ends[None, :]).astype(jnp.int32), axis=1), n_exp - 1)
    te = jnp.maximum(jnp.take(tiles_e, e), 1)
    local = s - ncol * jnp.take(ends - tiles_e, e)
    col = local // te
    tile = jnp.take(ends - tiles_e, e) + local % te
    valid = s < n_valid
    first = valid & (local % te == 0)
    nxt = jnp.minimum(s + te, ncol * n_tiles - 1)
    more = (s + te) < n_valid
    rem = s - n_valid
    tile = jnp.where(valid, tile, ends[-1] + rem // ncol)
    col = jnp.where(valid, col, rem % ncol)
    kind = jnp.where(valid, jnp.where(first, jnp.where(more, ITEM_FIRST_MORE, ITEM_FIRST), ITEM_COMPUTE),
                     ITEM_ZERO)
    rows = jnp.take(counts, e) - (tile - jnp.take(ends - tiles_e, e)) * tm_e
    half = (valid & (rows <= tm_e // 2)).astype(jnp.int32)
    return tile, col, e, kind.astype(jnp.int32), jnp.take(e, nxt), jnp.take(col, nxt), half


def _stage_weights(w_hbm_list, stage, wb, sem, kind, s, e_ref, c_ref, nxe_ref, nxc_ref, tn):
    def wcopy(e, c, k):
        cols = pl.ds(pl.multiple_of(c * tn, tn), tn)
        return pltpu.make_async_copy(w_hbm_list[k].at[e, :, cols], stage.at[k], sem.at[k])

    @pl.when(s == 0)
    def _():
        for k in range(len(w_hbm_list)):
            wcopy(e_ref[0], c_ref[0], k).start()

    @pl.when(kind >= ITEM_FIRST)
    def _():
        for k in range(len(w_hbm_list)):
            wcopy(e_ref[s], c_ref[s], k).wait()
            _cast_rows(stage.at[k], wb.at[k])

        @pl.when(kind == ITEM_FIRST_MORE)
        def _():
            for k in range(len(w_hbm_list)):
                wcopy(nxe_ref[s], nxc_ref[s], k).start()


def _row_tile_cases(kind, half, o_ref, compute):
    tm = o_ref.shape[0]

    @pl.when((kind >= ITEM_COMPUTE) & (half == 0))
    def _():
        compute(slice(0, tm))

    @pl.when((kind >= ITEM_COMPUTE) & (half > 0))
    def _():
        compute(slice(0, tm // 2))
        o_ref[tm // 2:, :] = jnp.zeros((tm - tm // 2, o_ref.shape[1]), o_ref.dtype)

    @pl.when(kind == ITEM_ZERO)
    def _():
        o_ref[...] = jnp.zeros(o_ref.shape, o_ref.dtype)


def _gm1_kernel(tile_ref, c_ref, e_ref, kind_ref, nxe_ref, nxc_ref, half_ref, x_ref, w1_hbm, w3_hbm, o_ref,
                stage, wb, sem, *, tf):
    s = pl.program_id(0)
    kind = kind_ref[s]
    _stage_weights((w1_hbm, w3_hbm), stage, wb, sem, kind, s, e_ref, c_ref, nxe_ref, nxc_ref, tf)

    def compute(rs):
        x = x_ref[rs, :].astype(BF16)
        a = jnp.dot(x, wb[0], preferred_element_type=F32)
        b = jnp.dot(x, wb[1], preferred_element_type=F32)
        o_ref[rs, :] = (jax.nn.silu(a) * b).astype(o_ref.dtype)

    _row_tile_cases(kind, half_ref[s], o_ref, compute)


def _gm2_kernel(tile_ref, c_ref, e_ref, kind_ref, nxe_ref, nxc_ref, half_ref, a_ref, w_hbm, o_ref,
                stage, wb, sem, *, tn):
    s = pl.program_id(0)
    kind = kind_ref[s]
    _stage_weights((w_hbm,), stage, wb, sem, kind, s, e_ref, c_ref, nxe_ref, nxc_ref, tn)

    def compute(rs):
        o_ref[rs, :] = jnp.dot(a_ref[rs, :], wb[0], preferred_element_type=F32).astype(o_ref.dtype)

    _row_tile_cases(kind, half_ref[s], o_ref, compute)


def _expert_ffn(xs, ends, tiles_e, counts, w1, w3, w2, tm_e):
    r, d = xs.shape
    _, _, f = w1.shape
    n_tiles = r // tm_e
    tf = _tile(f, 512)
    tn = _tile(d, 1024)
    row_map = lambda s, tile, col, *_: (tile[s], 0)
    out_map = lambda s, tile, col, *_: (tile[s], col[s])

    a_s = pl.pallas_call(
        functools.partial(_gm1_kernel, tf=tf),
        out_shape=jax.ShapeDtypeStruct((r, f), BF16),
        grid_spec=pltpu.PrefetchScalarGridSpec(
            num_scalar_prefetch=7,
            grid=(n_tiles * (f // tf),),
            in_specs=[pl.BlockSpec((tm_e, d), row_map),
                      pl.BlockSpec(memory_space=pl.ANY), pl.BlockSpec(memory_space=pl.ANY)],
            out_specs=pl.BlockSpec((tm_e, tf), out_map),
            scratch_shapes=[pltpu.VMEM((2, d, tf), F32), pltpu.VMEM((2, d, tf), BF16),
                            pltpu.SemaphoreType.DMA((2,))],
        ),
        compiler_params=_params(1),
        name="moe_expert_up",
    )(*_work_items(ends, tiles_e, counts, tm_e, n_tiles, f // tf), xs, w1, w3)

    return pl.pallas_call(
        functools.partial(_gm2_kernel, tn=tn),
        out_shape=jax.ShapeDtypeStruct((r, d), F32),
        grid_spec=pltpu.PrefetchScalarGridSpec(
            num_scalar_prefetch=7,
            grid=(n_tiles * (d // tn),),
            in_specs=[pl.BlockSpec((tm_e, f), row_map),
                      pl.BlockSpec(memory_space=pl.ANY)],
            out_specs=pl.BlockSpec((tm_e, tn), out_map),
            scratch_shapes=[pltpu.VMEM((1, f, tn), F32), pltpu.VMEM((1, f, tn), BF16),
                            pltpu.SemaphoreType.DMA((1,))],
        ),
        compiler_params=_params(1),
        name="moe_expert_down",
    )(*_work_items(ends, tiles_e, counts, tm_e, n_tiles, d // tn), a_s, w2)


def _combine_kernel(pos_ref, ys_hbm, x_ref, meta_ref, mod_ref, g_ref, o_ref, buf, sem, *, tm, n_steps):
    i = pl.program_id(0)

    def row_copy(half, r, k, src_row):
        return pltpu.make_async_copy(ys_hbm.at[pl.ds(src_row, 1), :], buf.at[half, k, pl.ds(r, 1), :],
                                     sem.at[half, k])

    def issue(step):
        half = step % 2
        base = step * tm

        def body(r, carry):
            for k in range(2):
                row_copy(half, r, k, pos_ref[2 * (base + r) + k]).start()
            return carry
        lax.fori_loop(0, tm, body, 0)

    @pl.when(i == 0)
    def _():
        issue(i)

    @pl.when(i + 1 < n_steps)
    def _():
        issue(i + 1)

    half = i % 2

    def wait(r, carry):
        for k in range(2):
            row_copy(half, r, k, 0).wait()
        return carry
    lax.fori_loop(0, tm, wait, 0)

    y = meta_ref[:, 4:5] * buf[half, 0] + meta_ref[:, 5:6] * buf[half, 1]
    xn = x_ref[...] + mod_ref[2:3, :] * y
    var = jnp.mean(xn * xn, axis=-1, keepdims=True)
    o_ref[...] = (xn * lax.rsqrt(var + EPS)) * g_ref[...]


def _combine_final(ys, pos, x, meta, mods, li, seq, g_final):
    t, d = x.shape
    tm = _tile(seq, 128, 8)
    return pl.pallas_call(
        functools.partial(_combine_kernel, tm=tm, n_steps=t // tm),
        out_shape=jax.ShapeDtypeStruct((t, d), F32),
        grid_spec=pltpu.PrefetchScalarGridSpec(
            num_scalar_prefetch=1,
            grid=(t // tm,),
            in_specs=[pl.BlockSpec(memory_space=pl.ANY),
                      pl.BlockSpec((tm, d), lambda m, pos_ref: (m, 0)),
                      pl.BlockSpec((tm, LANE), lambda m, pos_ref: (m, 0)),
                      pl.BlockSpec((None, None, 3, d), lambda m, pos_ref: (li, m // (seq // tm), 0, 0)),
                      pl.BlockSpec((1, d), lambda m, pos_ref: (0, 0))],
            out_specs=pl.BlockSpec((tm, d), lambda m, pos_ref: (m, 0)),
            scratch_shapes=[pltpu.VMEM((2, 2, tm, d), F32), pltpu.SemaphoreType.DMA((2, 2))],
        ),
        compiler_params=_params(1),
        name="moe_combine_final_norm",
    )(pos, ys, x, meta, mods, g_final.reshape(1, d))


def _rot_half_cols(w):
    half = w.shape[-1] // 2
    return jnp.concatenate([-w[..., half:], w[..., :half]], axis=-1)


def _prep_w_in(w_in, q_lora, kv_lora, n_lat):
    d = w_in.shape[0]
    o1 = q_lora + kv_lora
    o2 = o1 + MLA_ROPE
    k_rope = w_in[:, o1:o2]
    lat = jnp.concatenate([w_in[:, :o1], k_rope, _rot_half_cols(k_rope)], axis=1).astype(BF16)
    lat = jnp.pad(lat, ((0, 0), (0, n_lat - lat.shape[1])))
    return lat, w_in[:, o2:].astype(BF16)


def _prep_w_uq(w_uq, heads):
    ql = w_uq.shape[0]
    w = w_uq.reshape(ql, heads, MLA_NOPE + MLA_ROPE)
    rope = w[..., MLA_NOPE:]
    return jnp.concatenate([w[..., :MLA_NOPE], rope, _rot_half_cols(rope)], axis=-1
                           ).reshape(ql, heads * QK_HEAD).astype(BF16)


def _prep_w_ukv(w_ukv, heads):
    kvl = w_ukv.shape[0]
    w = w_ukv.reshape(kvl, heads, MLA_NOPE + MLA_V)
    return jnp.concatenate([w[..., :MLA_NOPE].reshape(kvl, heads * MLA_NOPE),
                            w[..., MLA_NOPE:].reshape(kvl, heads * MLA_V)], axis=1).astype(BF16)


def kernel(x, c, positions, ada_w, ada_b, norm_g, even_w_in, even_q_norm_g, even_kv_norm_g, even_w_uq, even_w_ukv, even_conv_w, even_w_out, even_ffn_w1, even_ffn_w3, even_ffn_w2, odd_w_in, odd_conv_w, odd_conv_b, odd_gate_a_w, odd_gate_a_b, odd_gate_x_w, odd_gate_x_b, odd_lambda, odd_w_out, odd_router_w, odd_router_b, odd_exp_w1, odd_exp_w3, odd_exp_w2, final_norm_g):
    batch, seq, d = x.shape
    t = batch * seq
    q_lora = even_q_norm_g.shape[1]
    kv_lora = even_kv_norm_g.shape[1]
    heads = even_w_uq.shape[2] // (MLA_NOPE + MLA_ROPE)
    conv_dim = even_conv_w.shape[2]
    d_ff = even_ffn_w1.shape[2]
    n_exp = odd_router_w.shape[2]
    assert even_w_in.shape[0] == 1 and odd_w_in.shape[0] == 1, "one layer of each type"

    xf = x.reshape(t, d)
    mods = _ada_modulation(c, ada_w, ada_b)
    rt = _rope_table(positions)
    tm = _tile(seq, 1024, 16)
    tm_dual = _tile(seq, 512, 16)

    n_lat = _round_up(q_lora + kv_lora + LANE, 256)
    w_lat, w_cv = _prep_w_in(even_w_in[0], q_lora, kv_lora, n_lat)
    h = _norm_mod(xf, norm_g[0], mods, 0, seq)
    lat = _matmul([h], [w_lat], [0], _epi_store, [], [], [jax.ShapeDtypeStruct((t, n_lat), BF16)],
                  tm=tm, tn=_tile(n_lat, 1024), name="in_proj_0_latents")[0]
    cv = _matmul([h], [w_cv], [0], _epi_store, [], [], [jax.ShapeDtypeStruct((t, 3 * conv_dim), BF16)],
                 tm=tm, tn=_tile(3 * conv_dim, 1024), name="in_proj_0_conv")[0]
    q = _q_projection(lat, even_q_norm_g[0], rt, _prep_w_uq(even_w_uq[0], heads), heads, q_lora)
    kn, v, kr = _kv_projection(lat, even_kv_norm_g[0], rt, _prep_w_ukv(even_w_ukv[0], heads),
                               heads, q_lora, kv_lora)
    attn = _attention(q, kn, kr, v, batch, seq, heads)
    conv = _gated_conv(cv, even_conv_w[0], seq, conv_dim)
    xf = _ws_residual_matmul([attn, conv], even_w_out, xf, mods, 0, seq, tm=tm, tn=_tile(d, 512),
                             name="out_proj_0")

    f_pad = _round_up(d_ff, 512)
    w2 = jnp.concatenate([even_ffn_w2[0].astype(BF16), jnp.zeros((f_pad - d_ff, d), BF16)], axis=0)
    h = _norm_mod(xf, norm_g[1], mods, 1, seq)
    act = _ws_matmul([h], [even_ffn_w1, even_ffn_w3], [0, 0], d_ff, _epi_swiglu, [], [],
                     [jax.ShapeDtypeStruct((t, f_pad), BF16)], tm=tm_dual, tn=_tile(f_pad, 512),
                     name="ffn_up")[0]
    xf = _residual_matmul([act], w2, xf, mods, 1, seq, tm=tm, tn=_tile(d, 1024),
                          tk=_tile(f_pad, 3072), name="ffn_down")

    width = odd_conv_w.shape[2]
    h = _norm_mod(xf, norm_g[2], mods, 2, seq)
    gate_br, xb_pre = _ws_matmul([h], [odd_w_in, odd_w_in], [0, width], width, _epi_gelu_pair, [], [],
                                 [jax.ShapeDtypeStruct((t, width), BF16)] * 2, tm=tm_dual,
                                 tn=_tile(width, 512), name="in_proj_1")
    y = _rglru(xb_pre, gate_br, odd_conv_w[0], odd_conv_b[0], odd_gate_a_w[0].astype(BF16),
               odd_gate_a_b[0], odd_gate_x_w[0].astype(BF16), odd_gate_x_b[0], odd_lambda[0], batch, seq)
    xf = _ws_residual_matmul([y], odd_w_out, xf, mods, 2, seq, tm=tm, tn=_tile(d, 512), name="out_proj_1")

    tm_e = _tile(seq, 512, 16)
    n_rows = 2 * t + n_exp * tm_e
    n_tiles = n_rows // tm_e
    h32, meta, cnt = _router(xf, norm_g[3], mods, 3, seq, odd_router_w[0], odd_router_b[0])
    counts = cnt[0, :n_exp].astype(jnp.int32)
    tiles_e = (counts + tm_e - 1) // tm_e
    ends = jnp.cumsum(tiles_e)
    start_rows = (ends - tiles_e) * tm_e
    e_idx = meta[:, 0:2].astype(jnp.int32)
    pos = (jnp.take(start_rows, e_idx) + meta[:, 2:4].astype(jnp.int32)).reshape(2 * t)
    tid = jnp.arange(n_tiles, dtype=jnp.int32)
    group_last = jnp.any((tid[:, None] == ends[None, :] - 1) & (tiles_e[None, :] > 0), axis=1)
    zero_flag = (group_last | (tid >= ends[-1])).astype(jnp.int32)
    xs = _dispatch(h32, pos, zero_flag, n_rows, tm_e, seq)
    ys = _expert_ffn(xs, ends, tiles_e, counts, odd_exp_w1[0], odd_exp_w3[0], odd_exp_w2[0], tm_e)
    out = _combine_final(ys, pos, xf, meta, mods, 3, seq, final_norm_g)
    return out.reshape(batch, seq, d)
```

```python
import functools

import jax
import jax.numpy as jnp
from jax import lax
from jax.experimental import pallas as pl
from jax.experimental.pallas import tpu as pltpu

F32 = jnp.float32
BF16 = jnp.bfloat16

EPS = 1e-6
ROPE_THETA = 10000.0
LRU_C = 8.0
MLA_NOPE = 128
MLA_ROPE = 64
MLA_V = 128
QK_HEAD = MLA_NOPE + 2 * MLA_ROPE
LANE = 128
ADA_ROWS = 16
VMEM_LIMIT_BYTES = 56 * 1024 * 1024


def _params(grid_rank):
    return pltpu.CompilerParams(dimension_semantics=("arbitrary",) * grid_rank,
                                vmem_limit_bytes=VMEM_LIMIT_BYTES)


def _tile(dim, pref, mult=LANE):
    if dim <= pref:
        return dim
    t = (pref // mult) * mult
    while t > mult and dim % t:
        t -= mult
    assert dim % t == 0, (dim, pref)
    return t


def _round_up(x, m):
    return (x + m - 1) // m * m


CAST_ROWS = 256


def _cast_rows(src_ref, dst_ref):
    rows = src_ref.shape[0]
    step = CAST_ROWS if rows % CAST_ROWS == 0 else rows
    for r in range(0, rows, step):
        dst_ref[r:r + step, :] = src_ref[r:r + step, :].astype(dst_ref.dtype)


def _ada_kernel(c_ref, w_ref, b_ref, o_ref):
    c = c_ref[...]
    sc = (c * jax.nn.sigmoid(c)).astype(BF16)
    o_ref[...] = jnp.dot(sc, w_ref[...].astype(BF16), preferred_element_type=F32) + b_ref[...]


def _ada_modulation(c, ada_w, ada_b):
    nmod, d, d3 = ada_w.shape
    b = c.shape[0]
    tn = _tile(d3, 512)
    cp = jnp.zeros((ADA_ROWS, d), F32).at[:b].set(c)
    out = pl.pallas_call(
        _ada_kernel,
        out_shape=jax.ShapeDtypeStruct((nmod, ADA_ROWS, d3), F32),
        grid=(nmod, d3 // tn),
        in_specs=[
            pl.BlockSpec((ADA_ROWS, d), lambda i, j: (0, 0)),
            pl.BlockSpec((None, d, tn), lambda i, j: (i, 0, j)),
            pl.BlockSpec((None, 1, tn), lambda i, j: (i, 0, j)),
        ],
        out_specs=pl.BlockSpec((None, ADA_ROWS, tn), lambda i, j: (i, 0, j)),
        compiler_params=_params(2),
        name="ada_modulation",
    )(cp, ada_w, ada_b.reshape(nmod, 1, d3))
    return out[:, :b].reshape(nmod, b, 3, d)


def _mod_spec(li, d_blk, rows_per_batch_tiles, col_map=None):
    if col_map is None:
        return pl.BlockSpec((None, None, 3, d_blk),
                            lambda m, *_: (li, m // rows_per_batch_tiles, 0, 0))
    return pl.BlockSpec((None, None, 3, d_blk),
                        lambda m, n, *_: (li, m // rows_per_batch_tiles, 0, n))


NORM_CHUNK = 16


def _norm_rows(x, gain, shift):
    var = jnp.mean(x * x, axis=-1, keepdims=True)
    return (x * lax.rsqrt(var + EPS)) * gain + shift


def _norm_kernel(x_ref, g_ref, mod_ref, h_ref, *, tm):
    gain = g_ref[...] * (1.0 + mod_ref[1:2, :])
    shift = mod_ref[0:1, :]

    def body(r, carry):
        r0 = pl.multiple_of(r * NORM_CHUNK, NORM_CHUNK)
        x = x_ref[pl.ds(r0, NORM_CHUNK), :]
        h_ref[pl.ds(r0, NORM_CHUNK), :] = _norm_rows(x, gain, shift).astype(h_ref.dtype)
        return carry
    lax.fori_loop(0, tm // NORM_CHUNK, body, 0, unroll=2)


def _norm_mod(x, g, mods, li, seq):
    t, d = x.shape
    tm = _tile(seq, 512, NORM_CHUNK)
    return pl.pallas_call(
        functools.partial(_norm_kernel, tm=tm),
        out_shape=jax.ShapeDtypeStruct((t, d), BF16),
        grid=(t // tm,),
        in_specs=[
            pl.BlockSpec((tm, d), lambda m: (m, 0)),
            pl.BlockSpec((1, d), lambda m: (0, 0)),
            _mod_spec(li, d, seq // tm),
        ],
        out_specs=pl.BlockSpec((tm, d), lambda m: (m, 0)),
        compiler_params=_params(1),
        name=f"norm_mod_{li}",
    )(x, g.reshape(1, d), mods)


def _mm_kernel(*refs, n_lhs, n_w, n_extra, n_out, nk, epilogue, w_transposed=False):
    lhs = refs[:n_lhs]
    ws = refs[n_lhs:n_lhs + n_w]
    extra = refs[n_lhs + n_w:n_lhs + n_w + n_extra]
    outs = refs[n_lhs + n_w + n_extra:n_lhs + n_w + n_extra + n_out]
    accs = refs[n_lhs + n_w + n_extra + n_out:]

    def partial_product(w_ref):
        if w_transposed:
            return lax.dot_general(lhs[0][...], w_ref[...], (((1,), (1,)), ((), ())),
                                   preferred_element_type=F32)
        off, tot = 0, None
        for l_ref in lhs:
            kk = l_ref.shape[1]
            part = jnp.dot(l_ref[...], w_ref[off:off + kk, :], preferred_element_type=F32)
            tot = part if tot is None else tot + part
            off += kk
        return tot

    if nk == 1:
        _run_epilogue(epilogue, [partial_product(w) for w in ws], extra, outs)
        return

    k = pl.program_id(2)

    @pl.when(k == 0)
    def _():
        for acc, w in zip(accs, ws):
            acc[...] = partial_product(w)

    @pl.when(k > 0)
    def _():
        for acc, w in zip(accs, ws):
            acc[...] += partial_product(w)

    @pl.when(k == nk - 1)
    def _():
        _run_epilogue(epilogue, [acc[...] for acc in accs], extra, outs)


def _matmul(lhs_list, w_list, w_col_offsets, epilogue, extras, extra_specs, out_shapes,
            *, tm, tn, tk=None, w_transposed=False, name):
    t = lhs_list[0].shape[0]
    ktot = sum(l.shape[1] for l in lhs_list)
    n = out_shapes[0].shape[1]
    if w_transposed:
        assert tk is None and len(lhs_list) == 1
    if tk is None:
        nk = 1
        lhs_specs = [pl.BlockSpec((tm, l.shape[1]), lambda m, j, k: (m, 0)) for l in lhs_list]
        w_rows = ktot
    else:
        assert len(lhs_list) == 1 and ktot % tk == 0
        nk = ktot // tk
        lhs_specs = [pl.BlockSpec((tm, tk), lambda m, j, k: (m, k))]
        w_rows = tk
    if w_transposed:
        w_specs = [pl.BlockSpec((tn, ktot), functools.partial(lambda m, j, k, off: (j + off, 0), off=off))
                   for off in w_col_offsets]
    else:
        w_specs = [pl.BlockSpec((w_rows, tn), functools.partial(lambda m, j, k, off: (k, j + off), off=off))
                   for off in w_col_offsets]
    kern = functools.partial(_mm_kernel, n_lhs=len(lhs_list), n_w=len(w_list), n_extra=len(extras),
                             n_out=len(out_shapes), nk=nk, epilogue=epilogue, w_transposed=w_transposed)
    scratch = [pltpu.VMEM((tm, tn), F32) for _ in w_list] if nk > 1 else []
    return pl.pallas_call(
        kern,
        out_shape=out_shapes,
        grid=(t // tm, n // tn, nk),
        in_specs=lhs_specs + w_specs + list(extra_specs),
        out_specs=[pl.BlockSpec((tm, tn), lambda m, j, k: (m, j)) for _ in out_shapes],
        scratch_shapes=scratch,
        compiler_params=_params(3),
        name=name,
    )(*lhs_list, *w_list, *extras)


def _ws_kernel(*refs, n_lhs, n_w, n_extra, n_out, tn, nj, col_offsets, rem, epilogue):
    lhs = refs[:n_lhs]
    w_hbm = refs[n_lhs:n_lhs + n_w]
    extra = refs[n_lhs + n_w:n_lhs + n_w + n_extra]
    outs = refs[n_lhs + n_w + n_extra:n_lhs + n_w + n_extra + n_out]
    stage, wb, sem = refs[n_lhs + n_w + n_extra + n_out:]
    j = pl.program_id(0)
    m = pl.program_id(1)

    def wcopy(col_tile, k, width):
        cols = pl.ds(pl.multiple_of(col_tile * tn + col_offsets[k], LANE), width)
        dst = stage.at[k] if width == tn else stage.at[k, :, 0:width]
        return pltpu.make_async_copy(w_hbm[k].at[0, :, cols], dst, sem.at[k])

    def for_tile(col_tile, fn):
        if rem == tn:
            for k in range(n_w):
                fn(wcopy(col_tile, k, tn))
            return

        @pl.when(col_tile < nj - 1)
        def _():
            for k in range(n_w):
                fn(wcopy(col_tile, k, tn))

        @pl.when(col_tile == nj - 1)
        def _():
            for k in range(n_w):
                fn(wcopy(col_tile, k, rem))

    @pl.when((j == 0) & (m == 0))
    def _():
        for_tile(j, lambda c: c.start())

    @pl.when(m == 0)
    def _():
        for_tile(j, lambda c: c.wait())
        for k in range(n_w):
            _cast_rows(stage.at[k], wb.at[k])
        if rem != tn:
            @pl.when(j == nj - 1)
            def _():
                for k in range(n_w):
                    wb[k, :, rem:] = jnp.zeros((wb.shape[1], tn - rem), BF16)

        @pl.when(j + 1 < nj)
        def _():
            for_tile(j + 1, lambda c: c.start())

    accs = []
    for k in range(n_w):
        off, tot = 0, None
        for l_ref in lhs:
            kk = l_ref.shape[1]
            part = jnp.dot(l_ref[...], wb[k, off:off + kk, :], preferred_element_type=F32)
            tot = part if tot is None else tot + part
            off += kk
        accs.append(tot)
    _run_epilogue(epilogue, accs, extra, outs)


def _ws_matmul(lhs_list, w_list, col_offsets, n_valid, epilogue, extras, extra_specs, out_shapes,
               *, tm, tn, name):
    t = lhs_list[0].shape[0]
    ktot = sum(l.shape[1] for l in lhs_list)
    n_out = out_shapes[0].shape[1]
    nj = n_out // tn
    rem = n_valid - (nj - 1) * tn
    assert 0 < rem <= tn and (rem == tn or nj > 1)
    kern = functools.partial(_ws_kernel, n_lhs=len(lhs_list), n_w=len(w_list), n_extra=len(extras),
                             n_out=len(out_shapes), tn=tn, nj=nj, col_offsets=tuple(col_offsets), rem=rem,
                             epilogue=epilogue)
    return pl.pallas_call(
        kern,
        out_shape=out_shapes,
        grid=(nj, t // tm),
        in_specs=[pl.BlockSpec((tm, l.shape[1]), lambda j, m: (m, 0)) for l in lhs_list]
                 + [pl.BlockSpec(memory_space=pl.ANY) for _ in w_list] + list(extra_specs),
        out_specs=[pl.BlockSpec((tm, tn), lambda j, m: (m, j)) for _ in out_shapes],
        scratch_shapes=[pltpu.VMEM((len(w_list), ktot, tn), F32), pltpu.VMEM((len(w_list), ktot, tn), BF16),
                        pltpu.SemaphoreType.DMA((len(w_list),))],
        compiler_params=_params(2),
        name=name,
    )(*lhs_list, *w_list, *extras)


def _ws_residual_matmul(lhs_list, w, x, mods, li, seq, *, tm, tn, name):
    t, d = x.shape
    tiles_per_batch = seq // tm
    extras = [x, mods]
    extra_specs = [pl.BlockSpec((tm, tn), lambda j, m: (m, j)),
                   pl.BlockSpec((None, None, 3, tn), lambda j, m: (li, m // tiles_per_batch, 0, j))]
    return _ws_matmul(lhs_list, [w], [0], d, _epi_residual, extras, extra_specs,
                      [jax.ShapeDtypeStruct((t, d), F32)], tm=tm, tn=tn, name=name)[0]


EPILOGUE_ROWS = 128


def _run_epilogue(epilogue, accs, extra, outs):
    rows = accs[0].shape[0]
    step = EPILOGUE_ROWS if rows % EPILOGUE_ROWS == 0 else rows
    for r in range(0, rows, step):
        epilogue([a[r:r + step] for a in accs], extra, outs, slice(r, r + step))


def _epi_store(accs, extra, outs, rs):
    outs[0][rs, :] = accs[0].astype(outs[0].dtype)


def _epi_residual(accs, extra, outs, rs):
    x_ref, mod_ref = extra
    outs[0][rs, :] = x_ref[rs, :] + mod_ref[2:3, :] * accs[0]


def _epi_swiglu(accs, extra, outs, rs):
    a, b = accs
    outs[0][rs, :] = (jax.nn.silu(a) * b).astype(outs[0].dtype)


def _epi_gelu_pair(accs, extra, outs, rs):
    a, b = accs
    outs[0][rs, :] = jax.nn.gelu(a, approximate=True).astype(outs[0].dtype)
    outs[1][rs, :] = b.astype(outs[1].dtype)


def _residual_matmul(lhs_list, w, x, mods, li, seq, *, tm, tn, tk=None, name):
    t, d = x.shape
    extras = [x, mods]
    extra_specs = [pl.BlockSpec((tm, tn), lambda m, j, k: (m, j)),
                   _mod_spec(li, tn, seq // tm, col_map=True)]
    return _matmul(lhs_list, [w], [0], _epi_residual, extras, extra_specs,
                   [jax.ShapeDtypeStruct((t, d), F32)], tm=tm, tn=tn, tk=tk, name=name)[0]


def _rope_kernel(pos_ref, inv_ref, o_ref):
    ang = pos_ref[...].astype(F32) * inv_ref[...]
    lane = lax.broadcasted_iota(jnp.int32, ang.shape, 1)
    o_ref[...] = jnp.where(lane < MLA_ROPE, jnp.cos(ang), jnp.sin(ang))


def _rope_table(positions):
    t = positions.size
    tm = _tile(t, 1024, 8)
    half = MLA_ROPE // 2
    inv = 1.0 / (ROPE_THETA ** (jnp.arange(0, MLA_ROPE, 2, dtype=F32) / MLA_ROPE))
    inv4 = jnp.tile(inv, 4).reshape(1, 4 * half)
    return pl.pallas_call(
        _rope_kernel,
        out_shape=jax.ShapeDtypeStruct((t, LANE), F32),
        grid=(t // tm,),
        in_specs=[pl.BlockSpec((tm, 1), lambda m: (m, 0)),
                  pl.BlockSpec((1, LANE), lambda m: (0, 0))],
        out_specs=pl.BlockSpec((tm, LANE), lambda m: (m, 0)),
        compiler_params=_params(1),
        name="rope_table",
    )(positions.reshape(t, 1), inv4)


def _qproj_kernel(cq_ref, g_ref, rt_ref, w_ref, o_ref, nq_ref, *, heads_per_tile, scale):
    @pl.when(pl.program_id(1) == 0)
    def _():
        cq = cq_ref[...].astype(F32)
        var = jnp.mean(cq * cq, axis=-1, keepdims=True)
        nq_ref[...] = ((cq * lax.rsqrt(var + EPS)) * g_ref[...]).astype(BF16)

    res = jnp.dot(nq_ref[...], w_ref[...], preferred_element_type=F32)
    rt = rt_ref[...] * scale
    for hh in range(heads_per_tile):
        c0 = hh * QK_HEAD
        o_ref[:, c0:c0 + MLA_NOPE] = (res[:, c0:c0 + MLA_NOPE] * scale).astype(o_ref.dtype)
        o_ref[:, c0 + MLA_NOPE:c0 + QK_HEAD] = (res[:, c0 + MLA_NOPE:c0 + QK_HEAD] * rt).astype(o_ref.dtype)


def _q_projection(proj, g_q, rt, w_uq_p, heads, q_lora):
    t = proj.shape[0]
    tm = _tile(t, 1024, 16)
    hpt = min(heads, 4)
    tn = hpt * QK_HEAD
    scale = (MLA_NOPE + MLA_ROPE) ** -0.5
    return pl.pallas_call(
        functools.partial(_qproj_kernel, heads_per_tile=hpt, scale=scale),
        out_shape=jax.ShapeDtypeStruct((t, heads * QK_HEAD), BF16),
        grid=(t // tm, heads // hpt),
        in_specs=[
            pl.BlockSpec((tm, q_lora), lambda m, j: (m, 0)),
            pl.BlockSpec((1, q_lora), lambda m, j: (0, 0)),
            pl.BlockSpec((tm, LANE), lambda m, j: (m, 0)),
            pl.BlockSpec((q_lora, tn), lambda m, j: (0, j)),
        ],
        out_specs=pl.BlockSpec((tm, tn), lambda m, j: (m, j)),
        scratch_shapes=[pltpu.VMEM((tm, q_lora), BF16)],
        compiler_params=_params(2),
        name="q_projection",
    )(proj, g_q.reshape(1, q_lora), rt, w_uq_p)


def _kvproj_kernel(ckv_ref, kr_ref, g_ref, rt_ref, w_ref, kn_ref, v_ref, kro_ref, *, hv):
    ckv = ckv_ref[...].astype(F32)
    var = jnp.mean(ckv * ckv, axis=-1, keepdims=True)
    nkv = ((ckv * lax.rsqrt(var + EPS)) * g_ref[...]).astype(BF16)
    res = jnp.dot(nkv, w_ref[...], preferred_element_type=F32)
    kn_ref[...] = res[:, :hv].astype(kn_ref.dtype)
    v_ref[...] = res[:, hv:].astype(v_ref.dtype)
    kv = kr_ref[...].astype(F32) * rt_ref[...]
    kro_ref[...] = (kv + pltpu.roll(kv, MLA_ROPE, 1)).astype(kro_ref.dtype)


def _kv_projection(proj, g_kv, rt, w_ukv_p, heads, q_lora, kv_lora):
    t = proj.shape[0]
    tm = _tile(t, 512, 16)
    hv = heads * MLA_NOPE
    return pl.pallas_call(
        functools.partial(_kvproj_kernel, hv=hv),
        out_shape=[jax.ShapeDtypeStruct((t, hv), BF16), jax.ShapeDtypeStruct((t, hv), BF16),
                   jax.ShapeDtypeStruct((t, LANE), BF16)],
        grid=(t // tm,),
        in_specs=[
            pl.BlockSpec((tm, kv_lora), lambda m: (m, q_lora // kv_lora)),
            pl.BlockSpec((tm, LANE), lambda m: (m, (q_lora + kv_lora) // LANE)),
            pl.BlockSpec((1, kv_lora), lambda m: (0, 0)),
            pl.BlockSpec((tm, LANE), lambda m: (m, 0)),
            pl.BlockSpec((kv_lora, 2 * hv), lambda m: (0, 0)),
        ],
        out_specs=[pl.BlockSpec((tm, hv), lambda m: (m, 0)), pl.BlockSpec((tm, hv), lambda m: (m, 0)),
                   pl.BlockSpec((tm, LANE), lambda m: (m, 0))],
        compiler_params=_params(1),
        name="kv_projection",
    )(proj, proj, g_kv.reshape(1, kv_lora), rt, w_ukv_p)


def _attn_kernel(q_ref, kn_ref, kr_ref, v_ref, o_ref, *, tq, hpb):
    qi = pl.program_id(2)
    ones = jnp.ones((tq, MLA_V), BF16)

    def block(ki, carry, masked):
        ks = pl.multiple_of(ki * tq, tq)
        kr = kr_ref[pl.ds(ks, tq), :]
        new = []
        for hh in range(hpb):
            m, acc = carry[hh]
            q = q_ref[:, hh * QK_HEAD:(hh + 1) * QK_HEAD]
            k = jnp.concatenate([kn_ref[pl.ds(ks, tq), hh * MLA_NOPE:(hh + 1) * MLA_NOPE], kr], axis=1)
            s = lax.dot_general(q, k, (((1,), (1,)), ((), ())), preferred_element_type=F32)
            if masked:
                row = lax.broadcasted_iota(jnp.int32, s.shape, 0)
                col = lax.broadcasted_iota(jnp.int32, s.shape, 1)
                s = jnp.where(col <= row, s, -jnp.inf)
            m_new = jnp.maximum(m, jnp.max(s, axis=1, keepdims=True))
            alpha = jnp.exp(m - m_new)
            p = jnp.exp(s - m_new).astype(BF16)
            v1 = jnp.concatenate([v_ref[pl.ds(ks, tq), hh * MLA_V:(hh + 1) * MLA_V], ones], axis=1)
            acc = alpha * acc + jnp.dot(p, v1, preferred_element_type=F32)
            new.append((m_new, acc))
        return tuple(new)

    init = tuple((jnp.full((tq, 1), -jnp.inf, F32), jnp.zeros((tq, 2 * MLA_V), F32)) for _ in range(hpb))
    carry = lax.fori_loop(0, qi, lambda ki, c: block(ki, c, False), init)
    carry = block(qi, carry, True)
    for hh in range(hpb):
        acc = carry[hh][1]
        o_ref[:, hh * MLA_V:(hh + 1) * MLA_V] = (acc[:, :MLA_V] / acc[:, MLA_V:]).astype(o_ref.dtype)


def _attention(q, kn, kr, v, batch, seq, heads):
    t = q.shape[0]
    tq = _tile(seq, 512, 16)
    nq = seq // tq
    hpb = 2 if heads % 2 == 0 else 1
    return pl.pallas_call(
        functools.partial(_attn_kernel, tq=tq, hpb=hpb),
        out_shape=jax.ShapeDtypeStruct((t, heads * MLA_V), BF16),
        grid=(batch, heads // hpb, nq),
        in_specs=[
            pl.BlockSpec((tq, hpb * QK_HEAD), lambda b, h, i: (b * nq + i, h)),
            pl.BlockSpec((seq, hpb * MLA_NOPE), lambda b, h, i: (b, h)),
            pl.BlockSpec((seq, LANE), lambda b, h, i: (b, 0)),
            pl.BlockSpec((seq, hpb * MLA_V), lambda b, h, i: (b, h)),
        ],
        out_specs=pl.BlockSpec((tq, hpb * MLA_V), lambda b, h, i: (b * nq + i, h)),
        compiler_params=_params(3),
        name="mla_attention",
    )(q, kn, kr, v)


HALO = 16


def _conv3_kernel(gb_ref, gc_ref, u_ref, gch_ref, uh_ref, w_ref, o_ref, ext_ref, *, tm, tiles_per_seq, taps):
    first = (pl.program_id(0) % tiles_per_seq) == 0
    halo = gch_ref[...].astype(F32) * uh_ref[...].astype(F32)
    ext_ref[0:HALO, :] = jnp.where(first, 0.0, halo)
    ext_ref[HALO:, :] = gc_ref[...].astype(F32) * u_ref[...].astype(F32)
    y = None
    for j in range(taps):
        off = HALO - (taps - 1 - j)
        term = w_ref[j:j + 1, :] * ext_ref[off:off + tm, :]
        y = term if y is None else y + term
    o_ref[...] = (gb_ref[...].astype(F32) * y).astype(o_ref.dtype)


def _gated_conv(proj, conv_w, seq, conv_dim):
    t = proj.shape[0]
    taps = conv_w.shape[0]
    tm = _tile(seq, 512, HALO)
    tc = _tile(conv_dim, 512)
    ob, oc, ou = 0, conv_dim // tc, 2 * conv_dim // tc
    hb = tm // HALO

    def halo_map(off):
        return lambda m, c: (jnp.maximum(m * hb - 1, 0), off + c)

    return pl.pallas_call(
        functools.partial(_conv3_kernel, tm=tm, tiles_per_seq=seq // tm, taps=taps),
        out_shape=jax.ShapeDtypeStruct((t, conv_dim), BF16),
        grid=(t // tm, conv_dim // tc),
        in_specs=[
            pl.BlockSpec((tm, tc), lambda m, c: (m, ob + c)),
            pl.BlockSpec((tm, tc), lambda m, c: (m, oc + c)),
            pl.BlockSpec((tm, tc), lambda m, c: (m, ou + c)),
            pl.BlockSpec((HALO, tc), halo_map(oc)),
            pl.BlockSpec((HALO, tc), halo_map(ou)),
            pl.BlockSpec((taps, tc), lambda m, c: (0, c)),
        ],
        out_specs=pl.BlockSpec((tm, tc), lambda m, c: (m, c)),
        scratch_shapes=[pltpu.VMEM((tm + HALO, tc), F32)],
        compiler_params=_params(2),
        name="gated_conv3",
    )(proj, proj, proj, proj, proj, conv_w)


GROUP = 8


def _lru_kernel(xb_ref, gbr_ref, cw_ref, cb_ref, wa_ref, ba_ref, wx_ref, bx_ref, lam_ref, o_ref,
                ext_ref, a_ref, b_ref, hc_ref, *, tm, hd, hp, taps):
    @pl.when(pl.program_id(2) == 0)
    def _():
        ext_ref[0:GROUP, :] = jnp.zeros((GROUP, ext_ref.shape[1]), F32)
        hc_ref[...] = jnp.zeros(hc_ref.shape, F32)

    ext_ref[GROUP:, :] = xb_ref[...].astype(F32)
    xb = cb_ref[...]
    for j in range(taps):
        off = GROUP - (taps - 1 - j)
        xb = xb + cw_ref[j:j + 1, :] * ext_ref[off:off + tm, :]
    ext_ref[0:GROUP, :] = ext_ref[tm:tm + GROUP, :]

    xb16 = xb.astype(BF16)
    rs, is_ = [], []
    for h in range(hp):
        xh = xb16[:, h * hd:(h + 1) * hd]
        rs.append(jnp.dot(xh, wa_ref[h], preferred_element_type=F32))
        is_.append(jnp.dot(xh, wx_ref[h], preferred_element_type=F32))
    r = jax.nn.sigmoid(jnp.concatenate(rs, axis=1) + ba_ref[...])
    ig = jax.nn.sigmoid(jnp.concatenate(is_, axis=1) + bx_ref[...])
    lam = lam_ref[...]
    log_sig = -(jnp.maximum(-lam, 0.0) + jnp.log(1.0 + jnp.exp(-jnp.abs(lam))))
    log_a = (LRU_C * r) * log_sig
    a = jnp.exp(log_a)
    th = jnp.tanh(log_a)
    mult = jnp.sqrt(-2.0 * th / (1.0 - th))
    a_ref[...] = a
    b_ref[...] = mult * (ig * xb)

    rowg = lax.broadcasted_iota(jnp.int32, (GROUP, a_ref.shape[1]), 0)

    def group(gi, hc):
        r0 = pl.multiple_of(gi * GROUP, GROUP)
        ag = a_ref[pl.ds(r0, GROUP), :]
        bg = b_ref[pl.ds(r0, GROUP), :]
        for dist in (1, 2, 4):
            keep = rowg >= dist
            ap = jnp.where(keep, pltpu.roll(ag, dist, 0), 1.0)
            bp = jnp.where(keep, pltpu.roll(bg, dist, 0), 0.0)
            bg = ag * bp + bg
            ag = ag * ap
        hs = ag * hc + bg
        y = hs * gbr_ref[pl.ds(r0, GROUP), :].astype(F32)
        b_ref[pl.ds(r0, GROUP), :] = y
        return jnp.broadcast_to(hs[GROUP - 1:GROUP, :], hs.shape)

    hc_ref[...] = lax.fori_loop(0, tm // GROUP, group, hc_ref[...], unroll=4)
    o_ref[...] = b_ref[...].astype(o_ref.dtype)


def _rglru(xb_pre, gate_br, conv_w, conv_b, wa, ba, wx, bx, lam, batch, seq):
    t, width = xb_pre.shape
    heads, hd, _ = wa.shape
    taps = conv_w.shape[0]
    hp = 2 if heads % 2 == 0 else 1
    c = hp * hd
    tm = _tile(seq, 512, 16)
    nt = seq // tm
    row = lambda b, h, i: (b * nt + i, h)
    vec = lambda b, h, i: (0, h)
    return pl.pallas_call(
        functools.partial(_lru_kernel, tm=tm, hd=hd, hp=hp, taps=taps),
        out_shape=jax.ShapeDtypeStruct((t, width), BF16),
        grid=(batch, heads // hp, nt),
        in_specs=[
            pl.BlockSpec((tm, c), row),
            pl.BlockSpec((tm, c), row),
            pl.BlockSpec((taps, c), vec),
            pl.BlockSpec((1, c), vec),
            pl.BlockSpec((hp, hd, hd), lambda b, h, i: (h, 0, 0)),
            pl.BlockSpec((1, c), vec),
            pl.BlockSpec((hp, hd, hd), lambda b, h, i: (h, 0, 0)),
            pl.BlockSpec((1, c), vec),
            pl.BlockSpec((1, c), vec),
        ],
        out_specs=pl.BlockSpec((tm, c), row),
        scratch_shapes=[pltpu.VMEM((tm + GROUP, c), F32), pltpu.VMEM((tm, c), F32),
                        pltpu.VMEM((tm, c), F32), pltpu.VMEM((GROUP, c), F32)],
        compiler_params=_params(3),
        name="rglru",
    )(xb_pre, gate_br, conv_w, conv_b.reshape(1, width), wa, ba.reshape(1, width), wx,
      bx.reshape(1, width), lam.reshape(1, width))


def _router_kernel(x_ref, g_ref, mod_ref, rw_ref, rb_ref, h_ref, meta_ref, cnt_ref, run_ref, hi_ref, lo_ref,
                   *, tm, n_exp):
    @pl.when(pl.program_id(0) == 0)
    def _():
        run_ref[...] = jnp.zeros(run_ref.shape, F32)

    gain = g_ref[...] * (1.0 + mod_ref[1:2, :])
    shift = mod_ref[0:1, :]

    def body(r, carry):
        r0 = pl.multiple_of(r * NORM_CHUNK, NORM_CHUNK)
        h = _norm_rows(x_ref[pl.ds(r0, NORM_CHUNK), :], gain, shift)
        hi = h.astype(BF16)
        h_ref[pl.ds(r0, NORM_CHUNK), :] = h
        hi_ref[pl.ds(r0, NORM_CHUNK), :] = hi
        lo_ref[pl.ds(r0, NORM_CHUNK), :] = (h - hi.astype(F32)).astype(BF16)
        return carry
    lax.fori_loop(0, tm // NORM_CHUNK, body, 0, unroll=2)

    logits = (jnp.dot(hi_ref[...], rw_ref[0], preferred_element_type=F32)
              + jnp.dot(lo_ref[...], rw_ref[0], preferred_element_type=F32)
              + jnp.dot(hi_ref[...], rw_ref[1], preferred_element_type=F32)) + rb_ref[...]
    lane = lax.broadcasted_iota(jnp.int32, logits.shape, 1)
    lg = jnp.where(lane < n_exp, logits, -jnp.inf)
    m1 = jnp.max(lg, axis=1, keepdims=True)
    i1 = jnp.min(jnp.where(lg == m1, lane, LANE), axis=1, keepdims=True)
    lg2 = jnp.where(lane == i1, -jnp.inf, lg)
    m2 = jnp.max(lg2, axis=1, keepdims=True)
    i2 = jnp.min(jnp.where(lg2 == m2, lane, LANE), axis=1, keepdims=True)
    e2 = jnp.exp(m2 - m1)
    den = 1.0 + e2
    w1 = 1.0 / den
    w2 = e2 / den

    hit1 = lane == i1
    hit2 = lane == i2
    sel = jnp.where(hit1 | hit2, 1.0, 0.0)
    rowi = lax.broadcasted_iota(jnp.int32, (tm, tm), 0)
    coli = lax.broadcasted_iota(jnp.int32, (tm, tm), 1)
    earlier = jnp.where(coli < rowi, 1.0, 0.0).astype(BF16)
    rank = jnp.dot(earlier, sel.astype(BF16), preferred_element_type=F32) + run_ref[0:1, :]
    r1 = jnp.sum(jnp.where(hit1, rank, 0.0), axis=1, keepdims=True)
    r2 = jnp.sum(jnp.where(hit2, rank, 0.0), axis=1, keepdims=True)
    run_ref[...] = run_ref[...] + jnp.sum(sel, axis=0, keepdims=True)
    cnt_ref[...] = run_ref[...]

    meta = jnp.where(lane == 0, i1.astype(F32), 0.0)
    meta = jnp.where(lane == 1, i2.astype(F32), meta)
    meta = jnp.where(lane == 2, r1, meta)
    meta = jnp.where(lane == 3, r2, meta)
    meta = jnp.where(lane == 4, w1, meta)
    meta = jnp.where(lane == 5, w2, meta)
    meta_ref[...] = meta


def _router(x, g, mods, li, seq, router_w, router_b):
    t, d = x.shape
    n_exp = router_w.shape[1]
    tm = _tile(seq, 256, 16)
    rw = jnp.zeros((d, LANE), F32).at[:, :n_exp].set(router_w)
    rw_hi = rw.astype(BF16)
    rw = jnp.stack([rw_hi, (rw - rw_hi.astype(F32)).astype(BF16)])
    rb = jnp.zeros((1, LANE), F32).at[0, :n_exp].set(router_b)
    return pl.pallas_call(
        functools.partial(_router_kernel, tm=tm, n_exp=n_exp),
        out_shape=[jax.ShapeDtypeStruct((t, d), F32), jax.ShapeDtypeStruct((t, LANE), F32),
                   jax.ShapeDtypeStruct((GROUP, LANE), F32)],
        grid=(t // tm,),
        in_specs=[
            pl.BlockSpec((tm, d), lambda m: (m, 0)),
            pl.BlockSpec((1, d), lambda m: (0, 0)),
            _mod_spec(li, d, seq // tm),
            pl.BlockSpec((2, d, LANE), lambda m: (0, 0, 0)),
            pl.BlockSpec((1, LANE), lambda m: (0, 0)),
        ],
        out_specs=[pl.BlockSpec((tm, d), lambda m: (m, 0)), pl.BlockSpec((tm, LANE), lambda m: (m, 0)),
                   pl.BlockSpec((GROUP, LANE), lambda m: (0, 0))],
        scratch_shapes=[pltpu.VMEM((GROUP, LANE), F32), pltpu.VMEM((tm, d), BF16), pltpu.VMEM((tm, d), BF16)],
        compiler_params=_params(1),
        name="moe_router",
    )(x, g.reshape(1, d), mods, rw, rb)


DMA_LOOP_UNROLL = 8


def _dispatch_kernel(pos_ref, zf_ref, h_ref, xs_hbm, zbuf, zsem, sem, *, tm, n_tiles, zr):
    @pl.when(pl.program_id(0) == 0)
    def _():
        zbuf[...] = jnp.zeros(zbuf.shape, zbuf.dtype)

        def zero_copy(c):
            return pltpu.make_async_copy(zbuf, xs_hbm.at[pl.ds(c * zr, zr), :], zsem)

        def zstart(c, carry):
            @pl.when(zf_ref[c] > 0)
            def _():
                zero_copy(c).start()
            return carry
        lax.fori_loop(0, n_tiles, zstart, 0)

        def zwait(c, carry):
            @pl.when(zf_ref[c] > 0)
            def _():
                zero_copy(c).wait()
            return carry
        lax.fori_loop(0, n_tiles, zwait, 0)

    base = pl.program_id(0) * tm

    def row_copy(r, slot):
        return pltpu.make_async_copy(h_ref.at[pl.ds(r, 1), :], xs_hbm.at[pl.ds(slot, 1), :], sem)

    def issue(r, carry):
        row_copy(r, pos_ref[2 * (base + r)]).start()
        row_copy(r, pos_ref[2 * (base + r) + 1]).start()
        return carry
    lax.fori_loop(0, tm, issue, 0, unroll=DMA_LOOP_UNROLL)

    def wait(r, carry):
        row_copy(r, 0).wait()
        row_copy(r, 0).wait()
        return carry
    lax.fori_loop(0, tm, wait, 0, unroll=DMA_LOOP_UNROLL)


def _dispatch(h, pos, zero_flag, n_rows, zr, seq):
    t, d = h.shape
    tm = _tile(seq, 256, 8)
    n_tiles = n_rows // zr
    return pl.pallas_call(
        functools.partial(_dispatch_kernel, tm=tm, n_tiles=n_tiles, zr=zr),
        out_shape=jax.ShapeDtypeStruct((n_rows, d), h.dtype),
        grid_spec=pltpu.PrefetchScalarGridSpec(
            num_scalar_prefetch=2,
            grid=(t // tm,),
            in_specs=[pl.BlockSpec((tm, d), lambda m, pos_ref, zf_ref: (m, 0))],
            out_specs=pl.BlockSpec(memory_space=pl.ANY),
            scratch_shapes=[pltpu.VMEM((zr, d), h.dtype), pltpu.SemaphoreType.DMA(()),
                            pltpu.SemaphoreType.DMA(())],
        ),
        compiler_params=_params(1),
        name="moe_dispatch",
    )(pos, zero_flag, h)


ITEM_ZERO, ITEM_COMPUTE, ITEM_FIRST, ITEM_FIRST_MORE = 0, 1, 2, 3


def _work_items(ends, tiles_e, counts, tm_e, n_tiles, ncol):
    n_exp = ends.shape[0]
    s = jnp.arange(ncol * n_tiles, dtype=jnp.int32)
    n_valid = ncol * ends[-1]
    e = jnp.minimum(jnp.sum((s[:, None] >= ncol * ends[None, :]).astype(jnp.int32), axis=1), n_exp - 1)
    te = jnp.maximum(jnp.take(tiles_e, e), 1)
    local = s - ncol * jnp.take(ends - tiles_e, e)
    col = local // te
    tile = jnp.take(ends - tiles_e, e) + local % te
    valid = s < n_valid
    first = valid & (local % te == 0)
    nxt = jnp.minimum(s + te, ncol * n_tiles - 1)
    more = (s + te) < n_valid
    rem = s - n_valid
    tile = jnp.where(valid, tile, ends[-1] + rem // ncol)
    col = jnp.where(valid, col, rem % ncol)
    kind = jnp.where(valid, jnp.where(first, jnp.where(more, ITEM_FIRST_MORE, ITEM_FIRST), ITEM_COMPUTE),
                     ITEM_ZERO)
    rows = jnp.take(counts, e) - (tile - jnp.take(ends - tiles_e, e)) * tm_e
    half = (valid & (rows <= tm_e // 2)).astype(jnp.int32)
    return tile, col, e, kind.astype(jnp.int32), jnp.take(e, nxt), jnp.take(col, nxt), half


def _stage_weights(w_hbm_list, stage, wb, sem, kind, s, e_ref, c_ref, nxe_ref, nxc_ref, tn):
    def wcopy(e, c, k):
        cols = pl.ds(pl.multiple_of(c * tn, tn), tn)
        return pltpu.make_async_copy(w_hbm_list[k].at[e, :, cols], stage.at[k], sem.at[k])

    @pl.when(s == 0)
    def _():
        for k in range(len(w_hbm_list)):
            wcopy(e_ref[0], c_ref[0], k).start()

    @pl.when(kind >= ITEM_FIRST)
    def _():
        for k in range(len(w_hbm_list)):
            wcopy(e_ref[s], c_ref[s], k).wait()
            _cast_rows(stage.at[k], wb.at[k])

        @pl.when(kind == ITEM_FIRST_MORE)
        def _():
            for k in range(len(w_hbm_list)):
                wcopy(nxe_ref[s], nxc_ref[s], k).start()


def _row_tile_cases(kind, half, o_ref, compute):
    tm = o_ref.shape[0]

    @pl.when((kind >= ITEM_COMPUTE) & (half == 0))
    def _():
        compute(slice(0, tm))

    @pl.when((kind >= ITEM_COMPUTE) & (half > 0))
    def _():
        compute(slice(0, tm // 2))
        o_ref[tm // 2:, :] = jnp.zeros((tm - tm // 2, o_ref.shape[1]), o_ref.dtype)

    @pl.when(kind == ITEM_ZERO)
    def _():
        o_ref[...] = jnp.zeros(o_ref.shape, o_ref.dtype)


def _gm1_kernel(tile_ref, c_ref, e_ref, kind_ref, nxe_ref, nxc_ref, half_ref, x_ref, w1_hbm, w3_hbm, o_ref,
                stage, wb, sem, *, tf):
    s = pl.program_id(0)
    kind = kind_ref[s]
    _stage_weights((w1_hbm, w3_hbm), stage, wb, sem, kind, s, e_ref, c_ref, nxe_ref, nxc_ref, tf)

    def compute(rs):
        x = x_ref[rs, :].astype(BF16)
        a = jnp.dot(x, wb[0], preferred_element_type=F32)
        b = jnp.dot(x, wb[1], preferred_element_type=F32)
        o_ref[rs, :] = (jax.nn.silu(a) * b).astype(o_ref.dtype)

    _row_tile_cases(kind, half_ref[s], o_ref, compute)


def _gm2_kernel(tile_ref, c_ref, e_ref, kind_ref, nxe_ref, nxc_ref, half_ref, a_ref, w_hbm, o_ref,
                stage, wb, sem, *, tn):
    s = pl.program_id(0)
    kind = kind_ref[s]
    _stage_weights((w_hbm,), stage, wb, sem, kind, s, e_ref, c_ref, nxe_ref, nxc_ref, tn)

    def compute(rs):
        o_ref[rs, :] = jnp.dot(a_ref[rs, :], wb[0], preferred_element_type=F32).astype(o_ref.dtype)

    _row_tile_cases(kind, half_ref[s], o_ref, compute)


def _expert_ffn(xs, ends, tiles_e, counts, w1, w3, w2, tm_e):
    r, d = xs.shape
    _, _, f = w1.shape
    n_tiles = r // tm_e
    tf = _tile(f, 512)
    tn = _tile(d, 1024)
    row_map = lambda s, tile, col, *_: (tile[s], 0)
    out_map = lambda s, tile, col, *_: (tile[s], col[s])

    a_s = pl.pallas_call(
        functools.partial(_gm1_kernel, tf=tf),
        out_shape=jax.ShapeDtypeStruct((r, f), BF16),
        grid_spec=pltpu.PrefetchScalarGridSpec(
            num_scalar_prefetch=7,
            grid=(n_tiles * (f // tf),),
            in_specs=[pl.BlockSpec((tm_e, d), row_map),
                      pl.BlockSpec(memory_space=pl.ANY), pl.BlockSpec(memory_space=pl.ANY)],
            out_specs=pl.BlockSpec((tm_e, tf), out_map),
            scratch_shapes=[pltpu.VMEM((2, d, tf), F32), pltpu.VMEM((2, d, tf), BF16),
                            pltpu.SemaphoreType.DMA((2,))],
        ),
        compiler_params=_params(1),
        name="moe_expert_up",
    )(*_work_items(ends, tiles_e, counts, tm_e, n_tiles, f // tf), xs, w1, w3)

    return pl.pallas_call(
        functools.partial(_gm2_kernel, tn=tn),
        out_shape=jax.ShapeDtypeStruct((r, d), F32),
        grid_spec=pltpu.PrefetchScalarGridSpec(
            num_scalar_prefetch=7,
            grid=(n_tiles * (d // tn),),
            in_specs=[pl.BlockSpec((tm_e, f), row_map),
                      pl.BlockSpec(memory_space=pl.ANY)],
            out_specs=pl.BlockSpec((tm_e, tn), out_map),
            scratch_shapes=[pltpu.VMEM((1, f, tn), F32), pltpu.VMEM((1, f, tn), BF16),
                            pltpu.SemaphoreType.DMA((1,))],
        ),
        compiler_params=_params(1),
        name="moe_expert_down",
    )(*_work_items(ends, tiles_e, counts, tm_e, n_tiles, d // tn), a_s, w2)


def _combine_kernel(pos_ref, ys_hbm, x_ref, meta_ref, mod_ref, g_ref, o_ref, buf, sem, *, tm, n_steps):
    i = pl.program_id(0)

    def row_copy(half, r, k, src_row):
        return pltpu.make_async_copy(ys_hbm.at[pl.ds(src_row, 1), :], buf.at[half, k, pl.ds(r, 1), :],
                                     sem.at[half, k])

    def issue(step):
        half = step % 2
        base = step * tm

        def body(r, carry):
            for k in range(2):
                row_copy(half, r, k, pos_ref[2 * (base + r) + k]).start()
            return carry
        lax.fori_loop(0, tm, body, 0, unroll=DMA_LOOP_UNROLL)

    @pl.when(i == 0)
    def _():
        issue(i)

    @pl.when(i + 1 < n_steps)
    def _():
        issue(i + 1)

    half = i % 2

    def wait(r, carry):
        for k in range(2):
            row_copy(half, r, k, 0).wait()
        return carry
    lax.fori_loop(0, tm, wait, 0, unroll=DMA_LOOP_UNROLL)

    y = meta_ref[:, 4:5] * buf[half, 0] + meta_ref[:, 5:6] * buf[half, 1]
    xn = x_ref[...] + mod_ref[2:3, :] * y
    var = jnp.mean(xn * xn, axis=-1, keepdims=True)
    o_ref[...] = (xn * lax.rsqrt(var + EPS)) * g_ref[...]


def _combine_final(ys, pos, x, meta, mods, li, seq, g_final):
    t, d = x.shape
    tm = _tile(seq, 128, 8)
    return pl.pallas_call(
        functools.partial(_combine_kernel, tm=tm, n_steps=t // tm),
        out_shape=jax.ShapeDtypeStruct((t, d), F32),
        grid_spec=pltpu.PrefetchScalarGridSpec(
            num_scalar_prefetch=1,
            grid=(t // tm,),
            in_specs=[pl.BlockSpec(memory_space=pl.ANY),
                      pl.BlockSpec((tm, d), lambda m, pos_ref: (m, 0)),
                      pl.BlockSpec((tm, LANE), lambda m, pos_ref: (m, 0)),
                      pl.BlockSpec((None, None, 3, d), lambda m, pos_ref: (li, m // (seq // tm), 0, 0)),
                      pl.BlockSpec((1, d), lambda m, pos_ref: (0, 0))],
            out_specs=pl.BlockSpec((tm, d), lambda m, pos_ref: (m, 0)),
            scratch_shapes=[pltpu.VMEM((2, 2, tm, d), F32), pltpu.SemaphoreType.DMA((2, 2))],
        ),
        compiler_params=_params(1),
        name="moe_combine_final_norm",
    )(pos, ys, x, meta, mods, g_final.reshape(1, d))


def _rot_half_cols(w):
    half = w.shape[-1] // 2
    return jnp.concatenate([-w[..., half:], w[..., :half]], axis=-1)


def _prep_w_in(w_in, q_lora, kv_lora, n_lat):
    w_t = jnp.swapaxes(w_in, 0, 1)
    d = w_t.shape[1]
    o1 = q_lora + kv_lora
    o2 = o1 + MLA_ROPE
    half = MLA_ROPE // 2
    k_rope = w_t[o1:o2]
    rot = jnp.concatenate([-k_rope[half:], k_rope[:half]], axis=0)
    pad = jnp.zeros((n_lat - o2 - MLA_ROPE, d), w_t.dtype)
    lat = jnp.concatenate([w_t[:o2], rot, pad], axis=0).astype(BF16)
    return lat, w_t[o2:].astype(BF16)


def _prep_w_uq(w_uq, heads):
    ql = w_uq.shape[0]
    w = w_uq.reshape(ql, heads, MLA_NOPE + MLA_ROPE)
    rope = w[..., MLA_NOPE:]
    return jnp.concatenate([w[..., :MLA_NOPE], rope, _rot_half_cols(rope)], axis=-1
                           ).reshape(ql, heads * QK_HEAD).astype(BF16)


def _prep_w_ukv(w_ukv, heads):
    kvl = w_ukv.shape[0]
    w = w_ukv.reshape(kvl, heads, MLA_NOPE + MLA_V)
    return jnp.concatenate([w[..., :MLA_NOPE].reshape(kvl, heads * MLA_NOPE),
                            w[..., MLA_NOPE:].reshape(kvl, heads * MLA_V)], axis=1).astype(BF16)


def kernel(x, c, positions, ada_w, ada_b, norm_g, even_w_in, even_q_norm_g, even_kv_norm_g, even_w_uq, even_w_ukv, even_conv_w, even_w_out, even_ffn_w1, even_ffn_w3, even_ffn_w2, odd_w_in, odd_conv_w, odd_conv_b, odd_gate_a_w, odd_gate_a_b, odd_gate_x_w, odd_gate_x_b, odd_lambda, odd_w_out, odd_router_w, odd_router_b, odd_exp_w1, odd_exp_w3, odd_exp_w2, final_norm_g):
    batch, seq, d = x.shape
    t = batch * seq
    q_lora = even_q_norm_g.shape[1]
    kv_lora = even_kv_norm_g.shape[1]
    heads = even_w_uq.shape[2] // (MLA_NOPE + MLA_ROPE)
    conv_dim = even_conv_w.shape[2]
    d_ff = even_ffn_w1.shape[2]
    n_exp = odd_router_w.shape[2]
    assert even_w_in.shape[0] == 1 and odd_w_in.shape[0] == 1, "one layer of each type"

    xf = x.reshape(t, d)
    mods = _ada_modulation(c, ada_w, ada_b)
    rt = _rope_table(positions)
    tm = _tile(seq, 1024, 16)
    tm_dual = _tile(seq, 512, 16)

    n_lat = _round_up(q_lora + kv_lora + LANE, 256)
    w_lat, w_cv = _prep_w_in(even_w_in[0], q_lora, kv_lora, n_lat)
    h = _norm_mod(xf, norm_g[0], mods, 0, seq)
    lat = _matmul([h], [w_lat], [0], _epi_store, [], [], [jax.ShapeDtypeStruct((t, n_lat), BF16)],
                  tm=tm, tn=_tile(n_lat, 1024), w_transposed=True, name="in_proj_0_latents")[0]
    cv = _matmul([h], [w_cv], [0], _epi_store, [], [], [jax.ShapeDtypeStruct((t, 3 * conv_dim), BF16)],
                 tm=tm, tn=_tile(3 * conv_dim, 1024), w_transposed=True, name="in_proj_0_conv")[0]
    q = _q_projection(lat, even_q_norm_g[0], rt, _prep_w_uq(even_w_uq[0], heads), heads, q_lora)
    kn, v, kr = _kv_projection(lat, even_kv_norm_g[0], rt, _prep_w_ukv(even_w_ukv[0], heads),
                               heads, q_lora, kv_lora)
    attn = _attention(q, kn, kr, v, batch, seq, heads)
    conv = _gated_conv(cv, even_conv_w[0], seq, conv_dim)
    xf = _ws_residual_matmul([attn, conv], even_w_out, xf, mods, 0, seq, tm=tm, tn=_tile(d, 512),
                             name="out_proj_0")

    f_pad = _round_up(d_ff, 512)
    w2 = jnp.concatenate([even_ffn_w2[0].astype(BF16), jnp.zeros((f_pad - d_ff, d), BF16)], axis=0)
    h = _norm_mod(xf, norm_g[1], mods, 1, seq)
    act = _ws_matmul([h], [even_ffn_w1, even_ffn_w3], [0, 0], d_ff, _epi_swiglu, [], [],
                     [jax.ShapeDtypeStruct((t, f_pad), BF16)], tm=tm_dual, tn=_tile(f_pad, 512),
                     name="ffn_up")[0]
    xf = _residual_matmul([act], w2, xf, mods, 1, seq, tm=tm, tn=_tile(d, 1024),
                          tk=_tile(f_pad, 3072), name="ffn_down")

    width = odd_conv_w.shape[2]
    h = _norm_mod(xf, norm_g[2], mods, 2, seq)
    gate_br, xb_pre = _ws_matmul([h], [odd_w_in, odd_w_in], [0, width], width, _epi_gelu_pair, [], [],
                                 [jax.ShapeDtypeStruct((t, width), BF16)] * 2, tm=tm_dual,
                                 tn=_tile(width, 512), name="in_proj_1")
    y = _rglru(xb_pre, gate_br, odd_conv_w[0], odd_conv_b[0], odd_gate_a_w[0].astype(BF16),
               odd_gate_a_b[0], odd_gate_x_w[0].astype(BF16), odd_gate_x_b[0], odd_lambda[0], batch, seq)
    xf = _ws_residual_matmul([y], odd_w_out, xf, mods, 2, seq, tm=tm, tn=_tile(d, 512), name="out_proj_1")

    tm_e = _tile(seq, 512, 16)
    n_rows = 2 * t + n_exp * tm_e
    n_tiles = n_rows // tm_e
    h32, meta, cnt = _router(xf, norm_g[3], mods, 3, seq, odd_router_w[0], odd_router_b[0])
    counts = cnt[0, :n_exp].astype(jnp.int32)
    tiles_e = (counts + tm_e - 1) // tm_e
    ends = jnp.cumsum(tiles_e)
    start_rows = (ends - tiles_e) * tm_e
    e_idx = meta[:, 0:2].astype(jnp.int32)
    pos = (jnp.take(start_rows, e_idx) + meta[:, 2:4].astype(jnp.int32)).reshape(2 * t)
    tid = jnp.arange(n_tiles, dtype=jnp.int32)
    group_last = jnp.any((tid[:, None] == ends[None, :] - 1) & (tiles_e[None, :] > 0), axis=1)
    zero_flag = (group_last | (tid >= ends[-1])).astype(jnp.int32)
    xs = _dispatch(h32, pos, zero_flag, n_rows, tm_e, seq)
    ys = _expert_ffn(xs, ends, tiles_e, counts, odd_exp_w1[0], odd_exp_w3[0], odd_exp_w2[0], tm_e)
    out = _combine_final(ys, pos, xf, meta, mods, 3, seq, final_norm_g)
    return out.reshape(batch, seq, d)
```

```python
import functools

import jax
import jax.numpy as jnp
from jax import lax
from jax.experimental import pallas as pl
from jax.experimental.pallas import tpu as pltpu

F32 = jnp.float32
BF16 = jnp.bfloat16

EPS = 1e-6
ROPE_THETA = 10000.0
LRU_C = 8.0
MLA_NOPE = 128
MLA_ROPE = 64
MLA_V = 128
QK_HEAD = MLA_NOPE + 2 * MLA_ROPE
LANE = 128
ADA_ROWS = 16
VMEM_LIMIT_BYTES = 56 * 1024 * 1024


def _params(grid_rank):
    return pltpu.CompilerParams(dimension_semantics=("arbitrary",) * grid_rank,
                                vmem_limit_bytes=VMEM_LIMIT_BYTES)


def _tile(dim, pref, mult=LANE):
    if dim <= pref:
        return dim
    t = (pref // mult) * mult
    while t > mult and dim % t:
        t -= mult
    assert dim % t == 0, (dim, pref)
    return t


def _round_up(x, m):
    return (x + m - 1) // m * m


CAST_ROWS = 256


def _cast_rows(src_ref, dst_ref):
    rows = src_ref.shape[0]
    step = CAST_ROWS if rows % CAST_ROWS == 0 else rows
    for r in range(0, rows, step):
        dst_ref[r:r + step, :] = src_ref[r:r + step, :].astype(dst_ref.dtype)


def _ada_kernel(c_ref, w_ref, b_ref, o_ref):
    c = c_ref[...]
    sc = (c * jax.nn.sigmoid(c)).astype(BF16)
    o_ref[...] = jnp.dot(sc, w_ref[...].astype(BF16), preferred_element_type=F32) + b_ref[...]


def _ada_modulation(c, ada_w, ada_b):
    nmod, d, d3 = ada_w.shape
    b = c.shape[0]
    tn = _tile(d3, 512)
    cp = jnp.zeros((ADA_ROWS, d), F32).at[:b].set(c)
    out = pl.pallas_call(
        _ada_kernel,
        out_shape=jax.ShapeDtypeStruct((nmod, ADA_ROWS, d3), F32),
        grid=(nmod, d3 // tn),
        in_specs=[
            pl.BlockSpec((ADA_ROWS, d), lambda i, j: (0, 0)),
            pl.BlockSpec((None, d, tn), lambda i, j: (i, 0, j)),
            pl.BlockSpec((None, 1, tn), lambda i, j: (i, 0, j)),
        ],
        out_specs=pl.BlockSpec((None, ADA_ROWS, tn), lambda i, j: (i, 0, j)),
        compiler_params=_params(2),
        name="ada_modulation",
    )(cp, ada_w, ada_b.reshape(nmod, 1, d3))
    return out[:, :b].reshape(nmod, b, 3, d)


def _mod_spec(li, d_blk, rows_per_batch_tiles, col_map=None):
    if col_map is None:
        return pl.BlockSpec((None, None, 3, d_blk),
                            lambda m, *_: (li, m // rows_per_batch_tiles, 0, 0))
    return pl.BlockSpec((None, None, 3, d_blk),
                        lambda m, n, *_: (li, m // rows_per_batch_tiles, 0, n))


NORM_CHUNK = 16


def _norm_rows(x, gain, shift):
    var = jnp.mean(x * x, axis=-1, keepdims=True)
    return (x * lax.rsqrt(var + EPS)) * gain + shift


def _norm_kernel(x_ref, g_ref, mod_ref, h_ref, *, tm):
    gain = g_ref[...] * (1.0 + mod_ref[1:2, :])
    shift = mod_ref[0:1, :]

    def body(r, carry):
        r0 = pl.multiple_of(r * NORM_CHUNK, NORM_CHUNK)
        x = x_ref[pl.ds(r0, NORM_CHUNK), :]
        h_ref[pl.ds(r0, NORM_CHUNK), :] = _norm_rows(x, gain, shift).astype(h_ref.dtype)
        return carry
    lax.fori_loop(0, tm // NORM_CHUNK, body, 0, unroll=2)


def _norm_mod(x, g, mods, li, seq):
    t, d = x.shape
    tm = _tile(seq, 512, NORM_CHUNK)
    return pl.pallas_call(
        functools.partial(_norm_kernel, tm=tm),
        out_shape=jax.ShapeDtypeStruct((t, d), BF16),
        grid=(t // tm,),
        in_specs=[
            pl.BlockSpec((tm, d), lambda m: (m, 0)),
            pl.BlockSpec((1, d), lambda m: (0, 0)),
            _mod_spec(li, d, seq // tm),
        ],
        out_specs=pl.BlockSpec((tm, d), lambda m: (m, 0)),
        compiler_params=_params(1),
        name=f"norm_mod_{li}",
    )(x, g.reshape(1, d), mods)


def _mm_kernel(*refs, n_lhs, n_w, n_extra, n_out, nk, epilogue, w_transposed=False):
    lhs = refs[:n_lhs]
    ws = refs[n_lhs:n_lhs + n_w]
    extra = refs[n_lhs + n_w:n_lhs + n_w + n_extra]
    outs = refs[n_lhs + n_w + n_extra:n_lhs + n_w + n_extra + n_out]
    accs = refs[n_lhs + n_w + n_extra + n_out:]

    def partial_product(w_ref):
        if w_transposed:
            return lax.dot_general(lhs[0][...], w_ref[...], (((1,), (1,)), ((), ())),
                                   preferred_element_type=F32)
        off, tot = 0, None
        for l_ref in lhs:
            kk = l_ref.shape[1]
            part = jnp.dot(l_ref[...], w_ref[off:off + kk, :], preferred_element_type=F32)
            tot = part if tot is None else tot + part
            off += kk
        return tot

    if nk == 1:
        _run_epilogue(epilogue, [partial_product(w) for w in ws], extra, outs)
        return

    k = pl.program_id(2)

    @pl.when(k == 0)
    def _():
        for acc, w in zip(accs, ws):
            acc[...] = partial_product(w)

    @pl.when(k > 0)
    def _():
        for acc, w in zip(accs, ws):
            acc[...] += partial_product(w)

    @pl.when(k == nk - 1)
    def _():
        _run_epilogue(epilogue, [acc[...] for acc in accs], extra, outs)


def _matmul(lhs_list, w_list, w_col_offsets, epilogue, extras, extra_specs, out_shapes,
            *, tm, tn, tk=None, w_transposed=False, name):
    t = lhs_list[0].shape[0]
    ktot = sum(l.shape[1] for l in lhs_list)
    n = out_shapes[0].shape[1]
    if w_transposed:
        assert tk is None and len(lhs_list) == 1
    if tk is None:
        nk = 1
        lhs_specs = [pl.BlockSpec((tm, l.shape[1]), lambda m, j, k: (m, 0)) for l in lhs_list]
        w_rows = ktot
    else:
        assert len(lhs_list) == 1 and ktot % tk == 0
        nk = ktot // tk
        lhs_specs = [pl.BlockSpec((tm, tk), lambda m, j, k: (m, k))]
        w_rows = tk
    if w_transposed:
        w_specs = [pl.BlockSpec((tn, ktot), functools.partial(lambda m, j, k, off: (j + off, 0), off=off))
                   for off in w_col_offsets]
    else:
        w_specs = [pl.BlockSpec((w_rows, tn), functools.partial(lambda m, j, k, off: (k, j + off), off=off))
                   for off in w_col_offsets]
    kern = functools.partial(_mm_kernel, n_lhs=len(lhs_list), n_w=len(w_list), n_extra=len(extras),
                             n_out=len(out_shapes), nk=nk, epilogue=epilogue, w_transposed=w_transposed)
    scratch = [pltpu.VMEM((tm, tn), F32) for _ in w_list] if nk > 1 else []
    return pl.pallas_call(
        kern,
        out_shape=out_shapes,
        grid=(t // tm, n // tn, nk),
        in_specs=lhs_specs + w_specs + list(extra_specs),
        out_specs=[pl.BlockSpec((tm, tn), lambda m, j, k: (m, j)) for _ in out_shapes],
        scratch_shapes=scratch,
        compiler_params=_params(3),
        name=name,
    )(*lhs_list, *w_list, *extras)


def _ws_kernel(*refs, n_lhs, n_w, n_extra, n_out, tn, nj, col_offsets, rem, epilogue):
    lhs = refs[:n_lhs]
    w_hbm = refs[n_lhs:n_lhs + n_w]
    extra = refs[n_lhs + n_w:n_lhs + n_w + n_extra]
    outs = refs[n_lhs + n_w + n_extra:n_lhs + n_w + n_extra + n_out]
    stage, wb, sem = refs[n_lhs + n_w + n_extra + n_out:]
    j = pl.program_id(0)
    m = pl.program_id(1)

    def wcopy(col_tile, k, width):
        cols = pl.ds(pl.multiple_of(col_tile * tn + col_offsets[k], LANE), width)
        dst = stage.at[k] if width == tn else stage.at[k, :, 0:width]
        return pltpu.make_async_copy(w_hbm[k].at[0, :, cols], dst, sem.at[k])

    def for_tile(col_tile, fn):
        if rem == tn:
            for k in range(n_w):
                fn(wcopy(col_tile, k, tn))
            return

        @pl.when(col_tile < nj - 1)
        def _():
            for k in range(n_w):
                fn(wcopy(col_tile, k, tn))

        @pl.when(col_tile == nj - 1)
        def _():
            for k in range(n_w):
                fn(wcopy(col_tile, k, rem))

    @pl.when((j == 0) & (m == 0))
    def _():
        for_tile(j, lambda c: c.start())

    @pl.when(m == 0)
    def _():
        for_tile(j, lambda c: c.wait())
        for k in range(n_w):
            _cast_rows(stage.at[k], wb.at[k])
        if rem != tn:
            @pl.when(j == nj - 1)
            def _():
                for k in range(n_w):
                    wb[k, :, rem:] = jnp.zeros((wb.shape[1], tn - rem), BF16)

        @pl.when(j + 1 < nj)
        def _():
            for_tile(j + 1, lambda c: c.start())

    accs = []
    for k in range(n_w):
        off, tot = 0, None
        for l_ref in lhs:
            kk = l_ref.shape[1]
            part = jnp.dot(l_ref[...], wb[k, off:off + kk, :], preferred_element_type=F32)
            tot = part if tot is None else tot + part
            off += kk
        accs.append(tot)
    _run_epilogue(epilogue, accs, extra, outs)


def _ws_matmul(lhs_list, w_list, col_offsets, n_valid, epilogue, extras, extra_specs, out_shapes,
               *, tm, tn, name):
    t = lhs_list[0].shape[0]
    ktot = sum(l.shape[1] for l in lhs_list)
    n_out = out_shapes[0].shape[1]
    nj = n_out // tn
    rem = n_valid - (nj - 1) * tn
    assert 0 < rem <= tn and (rem == tn or nj > 1)
    kern = functools.partial(_ws_kernel, n_lhs=len(lhs_list), n_w=len(w_list), n_extra=len(extras),
                             n_out=len(out_shapes), tn=tn, nj=nj, col_offsets=tuple(col_offsets), rem=rem,
                             epilogue=epilogue)
    return pl.pallas_call(
        kern,
        out_shape=out_shapes,
        grid=(nj, t // tm),
        in_specs=[pl.BlockSpec((tm, l.shape[1]), lambda j, m: (m, 0)) for l in lhs_list]
                 + [pl.BlockSpec(memory_space=pl.ANY) for _ in w_list] + list(extra_specs),
        out_specs=[pl.BlockSpec((tm, tn), lambda j, m: (m, j)) for _ in out_shapes],
        scratch_shapes=[pltpu.VMEM((len(w_list), ktot, tn), F32), pltpu.VMEM((len(w_list), ktot, tn), BF16),
                        pltpu.SemaphoreType.DMA((len(w_list),))],
        compiler_params=_params(2),
        name=name,
    )(*lhs_list, *w_list, *extras)


def _ws_residual_matmul(lhs_list, w, x, mods, li, seq, *, tm, tn, name):
    t, d = x.shape
    tiles_per_batch = seq // tm
    extras = [x, mods]
    extra_specs = [pl.BlockSpec((tm, tn), lambda j, m: (m, j)),
                   pl.BlockSpec((None, None, 3, tn), lambda j, m: (li, m // tiles_per_batch, 0, j))]
    return _ws_matmul(lhs_list, [w], [0], d, _epi_residual, extras, extra_specs,
                      [jax.ShapeDtypeStruct((t, d), F32)], tm=tm, tn=tn, name=name)[0]


EPILOGUE_ROWS = 128


def _run_epilogue(epilogue, accs, extra, outs):
    rows = accs[0].shape[0]
    step = EPILOGUE_ROWS if rows % EPILOGUE_ROWS == 0 else rows
    for r in range(0, rows, step):
        epilogue([a[r:r + step] for a in accs], extra, outs, slice(r, r + step))


def _epi_store(accs, extra, outs, rs):
    outs[0][rs, :] = accs[0].astype(outs[0].dtype)


def _epi_residual(accs, extra, outs, rs):
    x_ref, mod_ref = extra
    outs[0][rs, :] = x_ref[rs, :] + mod_ref[2:3, :] * accs[0]


def _epi_swiglu(accs, extra, outs, rs):
    a, b = accs
    outs[0][rs, :] = (jax.nn.silu(a) * b).astype(outs[0].dtype)


def _epi_gelu_pair(accs, extra, outs, rs):
    a, b = accs
    outs[0][rs, :] = jax.nn.gelu(a, approximate=True).astype(outs[0].dtype)
    outs[1][rs, :] = b.astype(outs[1].dtype)


def _residual_matmul(lhs_list, w, x, mods, li, seq, *, tm, tn, tk=None, name):
    t, d = x.shape
    extras = [x, mods]
    extra_specs = [pl.BlockSpec((tm, tn), lambda m, j, k: (m, j)),
                   _mod_spec(li, tn, seq // tm, col_map=True)]
    return _matmul(lhs_list, [w], [0], _epi_residual, extras, extra_specs,
                   [jax.ShapeDtypeStruct((t, d), F32)], tm=tm, tn=tn, tk=tk, name=name)[0]


def _rope_kernel(pos_ref, inv_ref, o_ref):
    ang = pos_ref[...].astype(F32) * inv_ref[...]
    lane = lax.broadcasted_iota(jnp.int32, ang.shape, 1)
    o_ref[...] = jnp.where(lane < MLA_ROPE, jnp.cos(ang), jnp.sin(ang))


def _rope_table(positions):
    t = positions.size
    tm = _tile(t, 1024, 8)
    half = MLA_ROPE // 2
    inv = 1.0 / (ROPE_THETA ** (jnp.arange(0, MLA_ROPE, 2, dtype=F32) / MLA_ROPE))
    inv4 = jnp.tile(inv, 4).reshape(1, 4 * half)
    return pl.pallas_call(
        _rope_kernel,
        out_shape=jax.ShapeDtypeStruct((t, LANE), F32),
        grid=(t // tm,),
        in_specs=[pl.BlockSpec((tm, 1), lambda m: (m, 0)),
                  pl.BlockSpec((1, LANE), lambda m: (0, 0))],
        out_specs=pl.BlockSpec((tm, LANE), lambda m: (m, 0)),
        compiler_params=_params(1),
        name="rope_table",
    )(positions.reshape(t, 1), inv4)


def _qproj_kernel(cq_ref, g_ref, rt_ref, w_ref, o_ref, nq_ref, *, heads_per_tile, scale):
    @pl.when(pl.program_id(1) == 0)
    def _():
        cq = cq_ref[...].astype(F32)
        var = jnp.mean(cq * cq, axis=-1, keepdims=True)
        nq_ref[...] = ((cq * lax.rsqrt(var + EPS)) * g_ref[...]).astype(BF16)

    res = jnp.dot(nq_ref[...], w_ref[...], preferred_element_type=F32)
    rt = rt_ref[...] * scale
    for hh in range(heads_per_tile):
        c0 = hh * QK_HEAD
        o_ref[:, c0:c0 + MLA_NOPE] = (res[:, c0:c0 + MLA_NOPE] * scale).astype(o_ref.dtype)
        o_ref[:, c0 + MLA_NOPE:c0 + QK_HEAD] = (res[:, c0 + MLA_NOPE:c0 + QK_HEAD] * rt).astype(o_ref.dtype)


def _q_projection(proj, g_q, rt, w_uq_p, heads, q_lora):
    t = proj.shape[0]
    tm = _tile(t, 1024, 16)
    hpt = min(heads, 4)
    tn = hpt * QK_HEAD
    scale = (MLA_NOPE + MLA_ROPE) ** -0.5
    return pl.pallas_call(
        functools.partial(_qproj_kernel, heads_per_tile=hpt, scale=scale),
        out_shape=jax.ShapeDtypeStruct((t, heads * QK_HEAD), BF16),
        grid=(t // tm, heads // hpt),
        in_specs=[
            pl.BlockSpec((tm, q_lora), lambda m, j: (m, 0)),
            pl.BlockSpec((1, q_lora), lambda m, j: (0, 0)),
            pl.BlockSpec((tm, LANE), lambda m, j: (m, 0)),
            pl.BlockSpec((q_lora, tn), lambda m, j: (0, j)),
        ],
        out_specs=pl.BlockSpec((tm, tn), lambda m, j: (m, j)),
        scratch_shapes=[pltpu.VMEM((tm, q_lora), BF16)],
        compiler_params=_params(2),
        name="q_projection",
    )(proj, g_q.reshape(1, q_lora), rt, w_uq_p)


def _kvproj_kernel(ckv_ref, kr_ref, g_ref, rt_ref, w_ref, kn_ref, v_ref, kro_ref, *, hv):
    ckv = ckv_ref[...].astype(F32)
    var = jnp.mean(ckv * ckv, axis=-1, keepdims=True)
    nkv = ((ckv * lax.rsqrt(var + EPS)) * g_ref[...]).astype(BF16)
    res = jnp.dot(nkv, w_ref[...], preferred_element_type=F32)
    kn_ref[...] = res[:, :hv].astype(kn_ref.dtype)
    v_ref[...] = res[:, hv:].astype(v_ref.dtype)
    kv = kr_ref[...].astype(F32) * rt_ref[...]
    kro_ref[...] = (kv + pltpu.roll(kv, MLA_ROPE, 1)).astype(kro_ref.dtype)


def _kv_projection(proj, g_kv, rt, w_ukv_p, heads, q_lora, kv_lora):
    t = proj.shape[0]
    tm = _tile(t, 512, 16)
    hv = heads * MLA_NOPE
    return pl.pallas_call(
        functools.partial(_kvproj_kernel, hv=hv),
        out_shape=[jax.ShapeDtypeStruct((t, hv), BF16), jax.ShapeDtypeStruct((t, hv), BF16),
                   jax.ShapeDtypeStruct((t, LANE), BF16)],
        grid=(t // tm,),
        in_specs=[
            pl.BlockSpec((tm, kv_lora), lambda m: (m, q_lora // kv_lora)),
            pl.BlockSpec((tm, LANE), lambda m: (m, (q_lora + kv_lora) // LANE)),
            pl.BlockSpec((1, kv_lora), lambda m: (0, 0)),
            pl.BlockSpec((tm, LANE), lambda m: (m, 0)),
            pl.BlockSpec((kv_lora, 2 * hv), lambda m: (0, 0)),
        ],
        out_specs=[pl.BlockSpec((tm, hv), lambda m: (m, 0)), pl.BlockSpec((tm, hv), lambda m: (m, 0)),
                   pl.BlockSpec((tm, LANE), lambda m: (m, 0))],
        compiler_params=_params(1),
        name="kv_projection",
    )(proj, proj, g_kv.reshape(1, kv_lora), rt, w_ukv_p)


def _attn_kernel(q_ref, kn_ref, kr_ref, v_ref, o_ref, *, tq, hpb):
    qi = pl.program_id(2)
    ones = jnp.ones((tq, MLA_V), BF16)

    def block(ki, carry, masked):
        ks = pl.multiple_of(ki * tq, tq)
        kr = kr_ref[pl.ds(ks, tq), :]
        new = []
        for hh in range(hpb):
            m, acc = carry[hh]
            q = q_ref[:, hh * QK_HEAD:(hh + 1) * QK_HEAD]
            k = jnp.concatenate([kn_ref[pl.ds(ks, tq), hh * MLA_NOPE:(hh + 1) * MLA_NOPE], kr], axis=1)
            s = lax.dot_general(q, k, (((1,), (1,)), ((), ())), preferred_element_type=F32)
            if masked:
                row = lax.broadcasted_iota(jnp.int32, s.shape, 0)
                col = lax.broadcasted_iota(jnp.int32, s.shape, 1)
                s = jnp.where(col <= row, s, -jnp.inf)
            m_new = jnp.maximum(m, jnp.max(s, axis=1, keepdims=True))
            alpha = jnp.exp(m - m_new)
            p = jnp.exp(s - m_new).astype(BF16)
            v1 = jnp.concatenate([v_ref[pl.ds(ks, tq), hh * MLA_V:(hh + 1) * MLA_V], ones], axis=1)
            acc = alpha * acc + jnp.dot(p, v1, preferred_element_type=F32)
            new.append((m_new, acc))
        return tuple(new)

    init = tuple((jnp.full((tq, 1), -jnp.inf, F32), jnp.zeros((tq, 2 * MLA_V), F32)) for _ in range(hpb))
    carry = lax.fori_loop(0, qi, lambda ki, c: block(ki, c, False), init)
    carry = block(qi, carry, True)
    for hh in range(hpb):
        acc = carry[hh][1]
        o_ref[:, hh * MLA_V:(hh + 1) * MLA_V] = (acc[:, :MLA_V] / acc[:, MLA_V:]).astype(o_ref.dtype)


def _attention(q, kn, kr, v, batch, seq, heads):
    t = q.shape[0]
    tq = _tile(seq, 512, 16)
    nq = seq // tq
    hpb = 2 if heads % 2 == 0 else 1
    return pl.pallas_call(
        functools.partial(_attn_kernel, tq=tq, hpb=hpb),
        out_shape=jax.ShapeDtypeStruct((t, heads * MLA_V), BF16),
        grid=(batch, heads // hpb, nq),
        in_specs=[
            pl.BlockSpec((tq, hpb * QK_HEAD), lambda b, h, i: (b * nq + i, h)),
            pl.BlockSpec((seq, hpb * MLA_NOPE), lambda b, h, i: (b, h)),
            pl.BlockSpec((seq, LANE), lambda b, h, i: (b, 0)),
            pl.BlockSpec((seq, hpb * MLA_V), lambda b, h, i: (b, h)),
        ],
        out_specs=pl.BlockSpec((tq, hpb * MLA_V), lambda b, h, i: (b * nq + i, h)),
        compiler_params=_params(3),
        name="mla_attention",
    )(q, kn, kr, v)


HALO = 16


def _conv3_kernel(gb_ref, gc_ref, u_ref, gch_ref, uh_ref, w_ref, o_ref, ext_ref, *, tm, tiles_per_seq, taps):
    first = (pl.program_id(0) % tiles_per_seq) == 0
    halo = gch_ref[...].astype(F32) * uh_ref[...].astype(F32)
    ext_ref[0:HALO, :] = jnp.where(first, 0.0, halo)
    ext_ref[HALO:, :] = gc_ref[...].astype(F32) * u_ref[...].astype(F32)
    y = None
    for j in range(taps):
        off = HALO - (taps - 1 - j)
        term = w_ref[j:j + 1, :] * ext_ref[off:off + tm, :]
        y = term if y is None else y + term
    o_ref[...] = (gb_ref[...].astype(F32) * y).astype(o_ref.dtype)


def _gated_conv(proj, conv_w, seq, conv_dim):
    t = proj.shape[0]
    taps = conv_w.shape[0]
    tm = _tile(seq, 512, HALO)
    tc = _tile(conv_dim, 512)
    ob, oc, ou = 0, conv_dim // tc, 2 * conv_dim // tc
    hb = tm // HALO

    def halo_map(off):
        return lambda m, c: (jnp.maximum(m * hb - 1, 0), off + c)

    return pl.pallas_call(
        functools.partial(_conv3_kernel, tm=tm, tiles_per_seq=seq // tm, taps=taps),
        out_shape=jax.ShapeDtypeStruct((t, conv_dim), BF16),
        grid=(t // tm, conv_dim // tc),
        in_specs=[
            pl.BlockSpec((tm, tc), lambda m, c: (m, ob + c)),
            pl.BlockSpec((tm, tc), lambda m, c: (m, oc + c)),
            pl.BlockSpec((tm, tc), lambda m, c: (m, ou + c)),
            pl.BlockSpec((HALO, tc), halo_map(oc)),
            pl.BlockSpec((HALO, tc), halo_map(ou)),
            pl.BlockSpec((taps, tc), lambda m, c: (0, c)),
        ],
        out_specs=pl.BlockSpec((tm, tc), lambda m, c: (m, c)),
        scratch_shapes=[pltpu.VMEM((tm + HALO, tc), F32)],
        compiler_params=_params(2),
        name="gated_conv3",
    )(proj, proj, proj, proj, proj, conv_w)


GROUP = 8


def _lru_kernel(xb_ref, gbr_ref, cw_ref, cb_ref, wa_ref, ba_ref, wx_ref, bx_ref, lam_ref, o_ref,
                ext_ref, a_ref, b_ref, hc_ref, *, tm, hd, hp, taps):
    @pl.when(pl.program_id(2) == 0)
    def _():
        ext_ref[0:GROUP, :] = jnp.zeros((GROUP, ext_ref.shape[1]), F32)
        hc_ref[...] = jnp.zeros(hc_ref.shape, F32)

    ext_ref[GROUP:, :] = xb_ref[...].astype(F32)
    ext = ext_ref[...]
    xb = cb_ref[...]
    for j in range(taps):
        back = taps - 1 - j
        shifted = ext if back == 0 else pltpu.roll(ext, back, 0)
        xb = xb + cw_ref[j:j + 1, :] * shifted[GROUP:GROUP + tm, :]
    ext_ref[0:GROUP, :] = ext_ref[tm:tm + GROUP, :]

    xb16 = xb.astype(BF16)
    rs, is_ = [], []
    for h in range(hp):
        xh = xb16[:, h * hd:(h + 1) * hd]
        rs.append(jnp.dot(xh, wa_ref[h], preferred_element_type=F32))
        is_.append(jnp.dot(xh, wx_ref[h], preferred_element_type=F32))
    r = jax.nn.sigmoid(jnp.concatenate(rs, axis=1) + ba_ref[...])
    ig = jax.nn.sigmoid(jnp.concatenate(is_, axis=1) + bx_ref[...])
    lam = lam_ref[...]
    log_sig = -(jnp.maximum(-lam, 0.0) + jnp.log(1.0 + jnp.exp(-jnp.abs(lam))))
    log_a = (LRU_C * r) * log_sig
    a = jnp.exp(log_a)
    th = jnp.tanh(log_a)
    mult = jnp.sqrt(-2.0 * th / (1.0 - th))
    a_ref[...] = a
    b_ref[...] = mult * (ig * xb)

    rowg = lax.broadcasted_iota(jnp.int32, (GROUP, a_ref.shape[1]), 0)

    def group(gi, hc):
        r0 = pl.multiple_of(gi * GROUP, GROUP)
        ag = a_ref[pl.ds(r0, GROUP), :]
        bg = b_ref[pl.ds(r0, GROUP), :]
        for dist in (1, 2, 4):
            keep = rowg >= dist
            ap = jnp.where(keep, pltpu.roll(ag, dist, 0), 1.0)
            bp = jnp.where(keep, pltpu.roll(bg, dist, 0), 0.0)
            bg = ag * bp + bg
            ag = ag * ap
        hs = ag * hc + bg
        y = hs * gbr_ref[pl.ds(r0, GROUP), :].astype(F32)
        b_ref[pl.ds(r0, GROUP), :] = y
        return jnp.broadcast_to(hs[GROUP - 1:GROUP, :], hs.shape)

    hc_ref[...] = lax.fori_loop(0, tm // GROUP, group, hc_ref[...], unroll=4)
    o_ref[...] = b_ref[...].astype(o_ref.dtype)


def _rglru(xb_pre, gate_br, conv_w, conv_b, wa, ba, wx, bx, lam, batch, seq):
    t, width = xb_pre.shape
    heads, hd, _ = wa.shape
    taps = conv_w.shape[0]
    hp = 4 if heads % 4 == 0 else (2 if heads % 2 == 0 else 1)
    c = hp * hd
    tm = _tile(seq, 512, 16)
    nt = seq // tm
    row = lambda b, h, i: (b * nt + i, h)
    vec = lambda b, h, i: (0, h)
    return pl.pallas_call(
        functools.partial(_lru_kernel, tm=tm, hd=hd, hp=hp, taps=taps),
        out_shape=jax.ShapeDtypeStruct((t, width), BF16),
        grid=(batch, heads // hp, nt),
        in_specs=[
            pl.BlockSpec((tm, c), row),
            pl.BlockSpec((tm, c), row),
            pl.BlockSpec((taps, c), vec),
            pl.BlockSpec((1, c), vec),
            pl.BlockSpec((hp, hd, hd), lambda b, h, i: (h, 0, 0)),
            pl.BlockSpec((1, c), vec),
            pl.BlockSpec((hp, hd, hd), lambda b, h, i: (h, 0, 0)),
            pl.BlockSpec((1, c), vec),
            pl.BlockSpec((1, c), vec),
        ],
        out_specs=pl.BlockSpec((tm, c), row),
        scratch_shapes=[pltpu.VMEM((tm + GROUP, c), F32), pltpu.VMEM((tm, c), F32),
                        pltpu.VMEM((tm, c), F32), pltpu.VMEM((GROUP, c), F32)],
        compiler_params=_params(3),
        name="rglru",
    )(xb_pre, gate_br, conv_w, conv_b.reshape(1, width), wa, ba.reshape(1, width), wx,
      bx.reshape(1, width), lam.reshape(1, width))


def _router_kernel(x_ref, g_ref, mod_ref, rw_ref, rb_ref, h_ref, meta_ref, cnt_ref, run_ref, hi_ref, lo_ref,
                   *, tm, n_exp):
    @pl.when(pl.program_id(0) == 0)
    def _():
        run_ref[...] = jnp.zeros(run_ref.shape, F32)

    gain = g_ref[...] * (1.0 + mod_ref[1:2, :])
    shift = mod_ref[0:1, :]

    def body(r, carry):
        r0 = pl.multiple_of(r * NORM_CHUNK, NORM_CHUNK)
        h = _norm_rows(x_ref[pl.ds(r0, NORM_CHUNK), :], gain, shift)
        hi = h.astype(BF16)
        h_ref[pl.ds(r0, NORM_CHUNK), :] = h
        hi_ref[pl.ds(r0, NORM_CHUNK), :] = hi
        lo_ref[pl.ds(r0, NORM_CHUNK), :] = (h - hi.astype(F32)).astype(BF16)
        return carry
    lax.fori_loop(0, tm // NORM_CHUNK, body, 0, unroll=2)

    logits = (jnp.dot(hi_ref[...], rw_ref[0], preferred_element_type=F32)
              + jnp.dot(lo_ref[...], rw_ref[0], preferred_element_type=F32)
              + jnp.dot(hi_ref[...], rw_ref[1], preferred_element_type=F32)) + rb_ref[...]
    lane = lax.broadcasted_iota(jnp.int32, logits.shape, 1)
    lg = jnp.where(lane < n_exp, logits, -jnp.inf)
    m1 = jnp.max(lg, axis=1, keepdims=True)
    i1 = jnp.min(jnp.where(lg == m1, lane, LANE), axis=1, keepdims=True)
    lg2 = jnp.where(lane == i1, -jnp.inf, lg)
    m2 = jnp.max(lg2, axis=1, keepdims=True)
    i2 = jnp.min(jnp.where(lg2 == m2, lane, LANE), axis=1, keepdims=True)
    e2 = jnp.exp(m2 - m1)
    den = 1.0 + e2
    w1 = 1.0 / den
    w2 = e2 / den

    hit1 = lane == i1
    hit2 = lane == i2
    sel = jnp.where(hit1 | hit2, 1.0, 0.0)
    rowi = lax.broadcasted_iota(jnp.int32, (tm, tm), 0)
    coli = lax.broadcasted_iota(jnp.int32, (tm, tm), 1)
    earlier = jnp.where(coli < rowi, 1.0, 0.0).astype(BF16)
    rank = jnp.dot(earlier, sel.astype(BF16), preferred_element_type=F32) + run_ref[0:1, :]
    r1 = jnp.sum(jnp.where(hit1, rank, 0.0), axis=1, keepdims=True)
    r2 = jnp.sum(jnp.where(hit2, rank, 0.0), axis=1, keepdims=True)
    run_ref[...] = run_ref[...] + jnp.sum(sel, axis=0, keepdims=True)
    cnt_ref[...] = run_ref[...]

    meta = jnp.where(lane == 0, i1.astype(F32), 0.0)
    meta = jnp.where(lane == 1, i2.astype(F32), meta)
    meta = jnp.where(lane == 2, r1, meta)
    meta = jnp.where(lane == 3, r2, meta)
    meta = jnp.where(lane == 4, w1, meta)
    meta = jnp.where(lane == 5, w2, meta)
    meta_ref[...] = meta


def _router(x, g, mods, li, seq, router_w, router_b):
    t, d = x.shape
    n_exp = router_w.shape[1]
    tm = _tile(seq, 256, 16)
    rw = jnp.zeros((d, LANE), F32).at[:, :n_exp].set(router_w)
    rw_hi = rw.astype(BF16)
    rw = jnp.stack([rw_hi, (rw - rw_hi.astype(F32)).astype(BF16)])
    rb = jnp.zeros((1, LANE), F32).at[0, :n_exp].set(router_b)
    return pl.pallas_call(
        functools.partial(_router_kernel, tm=tm, n_exp=n_exp),
        out_shape=[jax.ShapeDtypeStruct((t, d), F32), jax.ShapeDtypeStruct((t, LANE), F32),
                   jax.ShapeDtypeStruct((GROUP, LANE), F32)],
        grid=(t // tm,),
        in_specs=[
            pl.BlockSpec((tm, d), lambda m: (m, 0)),
            pl.BlockSpec((1, d), lambda m: (0, 0)),
            _mod_spec(li, d, seq // tm),
            pl.BlockSpec((2, d, LANE), lambda m: (0, 0, 0)),
            pl.BlockSpec((1, LANE), lambda m: (0, 0)),
        ],
        out_specs=[pl.BlockSpec((tm, d), lambda m: (m, 0)), pl.BlockSpec((tm, LANE), lambda m: (m, 0)),
                   pl.BlockSpec((GROUP, LANE), lambda m: (0, 0))],
        scratch_shapes=[pltpu.VMEM((GROUP, LANE), F32), pltpu.VMEM((tm, d), BF16), pltpu.VMEM((tm, d), BF16)],
        compiler_params=_params(1),
        name="moe_router",
    )(x, g.reshape(1, d), mods, rw, rb)


DMA_LOOP_UNROLL = 8


def _dispatch_kernel(pos_ref, zf_ref, h_ref, xs_hbm, zbuf, zsem, sem, *, tm, n_tiles, zr):
    @pl.when(pl.program_id(0) == 0)
    def _():
        zbuf[...] = jnp.zeros(zbuf.shape, zbuf.dtype)

        def zero_copy(c):
            return pltpu.make_async_copy(zbuf, xs_hbm.at[pl.ds(c * zr, zr), :], zsem)

        def zstart(c, carry):
            @pl.when(zf_ref[c] > 0)
            def _():
                zero_copy(c).start()
            return carry
        lax.fori_loop(0, n_tiles, zstart, 0)

        def zwait(c, carry):
            @pl.when(zf_ref[c] > 0)
            def _():
                zero_copy(c).wait()
            return carry
        lax.fori_loop(0, n_tiles, zwait, 0)

    base = pl.program_id(0) * tm

    def row_copy(r, slot):
        return pltpu.make_async_copy(h_ref.at[pl.ds(r, 1), :], xs_hbm.at[pl.ds(slot, 1), :], sem)

    def issue(r, carry):
        row_copy(r, pos_ref[2 * (base + r)]).start()
        row_copy(r, pos_ref[2 * (base + r) + 1]).start()
        return carry
    lax.fori_loop(0, tm, issue, 0, unroll=DMA_LOOP_UNROLL)

    def wait(r, carry):
        row_copy(r, 0).wait()
        row_copy(r, 0).wait()
        return carry
    lax.fori_loop(0, tm, wait, 0, unroll=DMA_LOOP_UNROLL)


def _dispatch(h, pos, zero_flag, n_rows, zr, seq):
    t, d = h.shape
    tm = _tile(seq, 256, 8)
    n_tiles = n_rows // zr
    return pl.pallas_call(
        functools.partial(_dispatch_kernel, tm=tm, n_tiles=n_tiles, zr=zr),
        out_shape=jax.ShapeDtypeStruct((n_rows, d), h.dtype),
        grid_spec=pltpu.PrefetchScalarGridSpec(
            num_scalar_prefetch=2,
            grid=(t // tm,),
            in_specs=[pl.BlockSpec((tm, d), lambda m, pos_ref, zf_ref: (m, 0))],
            out_specs=pl.BlockSpec(memory_space=pl.ANY),
            scratch_shapes=[pltpu.VMEM((zr, d), h.dtype), pltpu.SemaphoreType.DMA(()),
                            pltpu.SemaphoreType.DMA(())],
        ),
        compiler_params=_params(1),
        name="moe_dispatch",
    )(pos, zero_flag, h)


ITEM_ZERO, ITEM_COMPUTE, ITEM_FIRST, ITEM_FIRST_MORE = 0, 1, 2, 3


def _work_items(ends, tiles_e, counts, tm_e, n_tiles, ncol):
    n_exp = ends.shape[0]
    s = jnp.arange(ncol * n_tiles, dtype=jnp.int32)
    n_valid = ncol * ends[-1]
    e = jnp.minimum(jnp.sum((s[:, None] >= ncol * ends[None, :]).astype(jnp.int32), axis=1), n_exp - 1)
    te = jnp.maximum(jnp.take(tiles_e, e), 1)
    local = s - ncol * jnp.take(ends - tiles_e, e)
    col = local // te
    tile = jnp.take(ends - tiles_e, e) + local % te
    valid = s < n_valid
    first = valid & (local % te == 0)
    nxt = jnp.minimum(s + te, ncol * n_tiles - 1)
    more = (s + te) < n_valid
    rem = s - n_valid
    tile = jnp.where(valid, tile, ends[-1] + rem // ncol)
    col = jnp.where(valid, col, rem % ncol)
    kind = jnp.where(valid, jnp.where(first, jnp.where(more, ITEM_FIRST_MORE, ITEM_FIRST), ITEM_COMPUTE),
                     ITEM_ZERO)
    rows = jnp.take(counts, e) - (tile - jnp.take(ends - tiles_e, e)) * tm_e
    half = (valid & (rows <= tm_e // 2)).astype(jnp.int32)
    return tile, col, e, kind.astype(jnp.int32), jnp.take(e, nxt), jnp.take(col, nxt), half


def _stage_weights(w_hbm_list, stage, wb, sem, kind, s, e_ref, c_ref, nxe_ref, nxc_ref, tn):
    def wcopy(e, c, k):
        cols = pl.ds(pl.multiple_of(c * tn, tn), tn)
        return pltpu.make_async_copy(w_hbm_list[k].at[e, :, cols], stage.at[k], sem.at[k])

    @pl.when(s == 0)
    def _():
        for k in range(len(w_hbm_list)):
            wcopy(e_ref[0], c_ref[0], k).start()

    @pl.when(kind >= ITEM_FIRST)
    def _():
        for k in range(len(w_hbm_list)):
            wcopy(e_ref[s], c_ref[s], k).wait()
            _cast_rows(stage.at[k], wb.at[k])

        @pl.when(kind == ITEM_FIRST_MORE)
        def _():
            for k in range(len(w_hbm_list)):
                wcopy(nxe_ref[s], nxc_ref[s], k).start()


def _row_tile_cases(kind, half, o_ref, compute):
    tm = o_ref.shape[0]

    @pl.when((kind >= ITEM_COMPUTE) & (half == 0))
    def _():
        compute(slice(0, tm))

    @pl.when((kind >= ITEM_COMPUTE) & (half > 0))
    def _():
        compute(slice(0, tm // 2))
        o_ref[tm // 2:, :] = jnp.zeros((tm - tm // 2, o_ref.shape[1]), o_ref.dtype)

    @pl.when(kind == ITEM_ZERO)
    def _():
        o_ref[...] = jnp.zeros(o_ref.shape, o_ref.dtype)


def _gm1_kernel(tile_ref, c_ref, e_ref, kind_ref, nxe_ref, nxc_ref, half_ref, x_ref, w1_hbm, w3_hbm, o_ref,
                stage, wb, sem, *, tf):
    s = pl.program_id(0)
    kind = kind_ref[s]
    _stage_weights((w1_hbm, w3_hbm), stage, wb, sem, kind, s, e_ref, c_ref, nxe_ref, nxc_ref, tf)

    def compute(rs):
        x = x_ref[rs, :].astype(BF16)
        a = jnp.dot(x, wb[0], preferred_element_type=F32)
        b = jnp.dot(x, wb[1], preferred_element_type=F32)
        o_ref[rs, :] = (jax.nn.silu(a) * b).astype(o_ref.dtype)

    _row_tile_cases(kind, half_ref[s], o_ref, compute)


def _gm2_kernel(tile_ref, c_ref, e_ref, kind_ref, nxe_ref, nxc_ref, half_ref, a_ref, w_hbm, o_ref,
                stage, wb, sem, *, tn):
    s = pl.program_id(0)
    kind = kind_ref[s]
    _stage_weights((w_hbm,), stage, wb, sem, kind, s, e_ref, c_ref, nxe_ref, nxc_ref, tn)

    def compute(rs):
        o_ref[rs, :] = jnp.dot(a_ref[rs, :], wb[0], preferred_element_type=F32).astype(o_ref.dtype)

    _row_tile_cases(kind, half_ref[s], o_ref, compute)


def _expert_ffn(xs, ends, tiles_e, counts, w1, w3, w2, tm_e):
    r, d = xs.shape
    _, _, f = w1.shape
    n_tiles = r // tm_e
    tf = _tile(f, 512)
    tn = _tile(d, 1024)
    row_map = lambda s, tile, col, *_: (tile[s], 0)
    out_map = lambda s, tile, col, *_: (tile[s], col[s])

    a_s = pl.pallas_call(
        functools.partial(_gm1_kernel, tf=tf),
        out_shape=jax.ShapeDtypeStruct((r, f), BF16),
        grid_spec=pltpu.PrefetchScalarGridSpec(
            num_scalar_prefetch=7,
            grid=(n_tiles * (f // tf),),
            in_specs=[pl.BlockSpec((tm_e, d), row_map),
                      pl.BlockSpec(memory_space=pl.ANY), pl.BlockSpec(memory_space=pl.ANY)],
            out_specs=pl.BlockSpec((tm_e, tf), out_map),
            scratch_shapes=[pltpu.VMEM((2, d, tf), F32), pltpu.VMEM((2, d, tf), BF16),
                            pltpu.SemaphoreType.DMA((2,))],
        ),
        compiler_params=_params(1),
        name="moe_expert_up",
    )(*_work_items(ends, tiles_e, counts, tm_e, n_tiles, f // tf), xs, w1, w3)

    return pl.pallas_call(
        functools.partial(_gm2_kernel, tn=tn),
        out_shape=jax.ShapeDtypeStruct((r, d), F32),
        grid_spec=pltpu.PrefetchScalarGridSpec(
            num_scalar_prefetch=7,
            grid=(n_tiles * (d // tn),),
            in_specs=[pl.BlockSpec((tm_e, f), row_map),
                      pl.BlockSpec(memory_space=pl.ANY)],
            out_specs=pl.BlockSpec((tm_e, tn), out_map),
            scratch_shapes=[pltpu.VMEM((1, f, tn), F32), pltpu.VMEM((1, f, tn), BF16),
                            pltpu.SemaphoreType.DMA((1,))],
        ),
        compiler_params=_params(1),
        name="moe_expert_down",
    )(*_work_items(ends, tiles_e, counts, tm_e, n_tiles, d // tn), a_s, w2)


def _combine_kernel(pos_ref, ys_hbm, x_ref, meta_ref, mod_ref, g_ref, o_ref, buf, sem, *, tm, n_steps):
    i = pl.program_id(0)

    def row_copy(half, r, k, src_row):
        return pltpu.make_async_copy(ys_hbm.at[pl.ds(src_row, 1), :], buf.at[half, k, pl.ds(r, 1), :],
                                     sem.at[half, k])

    def issue(step):
        half = step % 2
        base = step * tm

        def body(r, carry):
            for k in range(2):
                row_copy(half, r, k, pos_ref[2 * (base + r) + k]).start()
            return carry
        lax.fori_loop(0, tm, body, 0, unroll=DMA_LOOP_UNROLL)

    @pl.when(i == 0)
    def _():
        issue(i)

    @pl.when(i + 1 < n_steps)
    def _():
        issue(i + 1)

    half = i % 2

    def wait(r, carry):
        for k in range(2):
            row_copy(half, r, k, 0).wait()
        return carry
    lax.fori_loop(0, tm, wait, 0, unroll=DMA_LOOP_UNROLL)

    y = meta_ref[:, 4:5] * buf[half, 0] + meta_ref[:, 5:6] * buf[half, 1]
    xn = x_ref[...] + mod_ref[2:3, :] * y
    var = jnp.mean(xn * xn, axis=-1, keepdims=True)
    o_ref[...] = (xn * lax.rsqrt(var + EPS)) * g_ref[...]


def _combine_final(ys, pos, x, meta, mods, li, seq, g_final):
    t, d = x.shape
    tm = _tile(seq, 128, 8)
    return pl.pallas_call(
        functools.partial(_combine_kernel, tm=tm, n_steps=t // tm),
        out_shape=jax.ShapeDtypeStruct((t, d), F32),
        grid_spec=pltpu.PrefetchScalarGridSpec(
            num_scalar_prefetch=1,
            grid=(t // tm,),
            in_specs=[pl.BlockSpec(memory_space=pl.ANY),
                      pl.BlockSpec((tm, d), lambda m, pos_ref: (m, 0)),
                      pl.BlockSpec((tm, LANE), lambda m, pos_ref: (m, 0)),
                      pl.BlockSpec((None, None, 3, d), lambda m, pos_ref: (li, m // (seq // tm), 0, 0)),
                      pl.BlockSpec((1, d), lambda m, pos_ref: (0, 0))],
            out_specs=pl.BlockSpec((tm, d), lambda m, pos_ref: (m, 0)),
            scratch_shapes=[pltpu.VMEM((2, 2, tm, d), F32), pltpu.SemaphoreType.DMA((2, 2))],
        ),
        compiler_params=_params(1),
        name="moe_combine_final_norm",
    )(pos, ys, x, meta, mods, g_final.reshape(1, d))


def _rot_half_cols(w):
    half = w.shape[-1] // 2
    return jnp.concatenate([-w[..., half:], w[..., :half]], axis=-1)


def _prep_w_in(w_in, q_lora, kv_lora, n_lat):
    w_t = jnp.swapaxes(w_in, 0, 1)
    d = w_t.shape[1]
    o1 = q_lora + kv_lora
    o2 = o1 + MLA_ROPE
    half = MLA_ROPE // 2
    k_rope = w_t[o1:o2]
    rot = jnp.concatenate([-k_rope[half:], k_rope[:half]], axis=0)
    pad = jnp.zeros((n_lat - o2 - MLA_ROPE, d), w_t.dtype)
    lat = jnp.concatenate([w_t[:o2], rot, pad], axis=0).astype(BF16)
    return lat, w_t[o2:].astype(BF16)


def _prep_w_uq(w_uq, heads):
    ql = w_uq.shape[0]
    w = w_uq.reshape(ql, heads, MLA_NOPE + MLA_ROPE)
    rope = w[..., MLA_NOPE:]
    return jnp.concatenate([w[..., :MLA_NOPE], rope, _rot_half_cols(rope)], axis=-1
                           ).reshape(ql, heads * QK_HEAD).astype(BF16)


def _prep_w_ukv(w_ukv, heads):
    kvl = w_ukv.shape[0]
    w = w_ukv.reshape(kvl, heads, MLA_NOPE + MLA_V)
    return jnp.concatenate([w[..., :MLA_NOPE].reshape(kvl, heads * MLA_NOPE),
                            w[..., MLA_NOPE:].reshape(kvl, heads * MLA_V)], axis=1).astype(BF16)


def kernel(x, c, positions, ada_w, ada_b, norm_g, even_w_in, even_q_norm_g, even_kv_norm_g, even_w_uq, even_w_ukv, even_conv_w, even_w_out, even_ffn_w1, even_ffn_w3, even_ffn_w2, odd_w_in, odd_conv_w, odd_conv_b, odd_gate_a_w, odd_gate_a_b, odd_gate_x_w, odd_gate_x_b, odd_lambda, odd_w_out, odd_router_w, odd_router_b, odd_exp_w1, odd_exp_w3, odd_exp_w2, final_norm_g):
    batch, seq, d = x.shape
    t = batch * seq
    q_lora = even_q_norm_g.shape[1]
    kv_lora = even_kv_norm_g.shape[1]
    heads = even_w_uq.shape[2] // (MLA_NOPE + MLA_ROPE)
    conv_dim = even_conv_w.shape[2]
    d_ff = even_ffn_w1.shape[2]
    n_exp = odd_router_w.shape[2]
    assert even_w_in.shape[0] == 1 and odd_w_in.shape[0] == 1, "one layer of each type"

    xf = x.reshape(t, d)
    mods = _ada_modulation(c, ada_w, ada_b)
    rt = _rope_table(positions)
    tm = _tile(seq, 1024, 16)
    tm_dual = _tile(seq, 512, 16)

    n_lat = _round_up(q_lora + kv_lora + LANE, 256)
    w_lat, w_cv = _prep_w_in(even_w_in[0], q_lora, kv_lora, n_lat)
    h = _norm_mod(xf, norm_g[0], mods, 0, seq)
    lat = _matmul([h], [w_lat], [0], _epi_store, [], [], [jax.ShapeDtypeStruct((t, n_lat), BF16)],
                  tm=tm, tn=_tile(n_lat, 1024), w_transposed=True, name="in_proj_0_latents")[0]
    cv = _matmul([h], [w_cv], [0], _epi_store, [], [], [jax.ShapeDtypeStruct((t, 3 * conv_dim), BF16)],
                 tm=tm, tn=_tile(3 * conv_dim, 1024), w_transposed=True, name="in_proj_0_conv")[0]
    q = _q_projection(lat, even_q_norm_g[0], rt, _prep_w_uq(even_w_uq[0], heads), heads, q_lora)
    kn, v, kr = _kv_projection(lat, even_kv_norm_g[0], rt, _prep_w_ukv(even_w_ukv[0], heads),
                               heads, q_lora, kv_lora)
    attn = _attention(q, kn, kr, v, batch, seq, heads)
    conv = _gated_conv(cv, even_conv_w[0], seq, conv_dim)
    xf = _ws_residual_matmul([attn, conv], even_w_out, xf, mods, 0, seq, tm=tm, tn=_tile(d, 512),
                             name="out_proj_0")

    f_pad = _round_up(d_ff, 512)
    w2 = jnp.concatenate([even_ffn_w2[0].astype(BF16), jnp.zeros((f_pad - d_ff, d), BF16)], axis=0)
    h = _norm_mod(xf, norm_g[1], mods, 1, seq)
    act = _ws_matmul([h], [even_ffn_w1, even_ffn_w3], [0, 0], d_ff, _epi_swiglu, [], [],
                     [jax.ShapeDtypeStruct((t, f_pad), BF16)], tm=tm_dual, tn=_tile(f_pad, 512),
                     name="ffn_up")[0]
    xf = _residual_matmul([act], w2, xf, mods, 1, seq, tm=tm, tn=_tile(d, 1024),
                          tk=_tile(f_pad, 3072), name="ffn_down")

    width = odd_conv_w.shape[2]
    h = _norm_mod(xf, norm_g[2], mods, 2, seq)
    gate_br, xb_pre = _ws_matmul([h], [odd_w_in, odd_w_in], [0, width], width, _epi_gelu_pair, [], [],
                                 [jax.ShapeDtypeStruct((t, width), BF16)] * 2, tm=tm_dual,
                                 tn=_tile(width, 512), name="in_proj_1")
    y = _rglru(xb_pre, gate_br, odd_conv_w[0], odd_conv_b[0], odd_gate_a_w[0].astype(BF16),
               odd_gate_a_b[0], odd_gate_x_w[0].astype(BF16), odd_gate_x_b[0], odd_lambda[0], batch, seq)
    xf = _ws_residual_matmul([y], odd_w_out, xf, mods, 2, seq, tm=tm, tn=_tile(d, 512), name="out_proj_1")

    tm_e = _tile(seq, 512, 16)
    n_rows = 2 * t + n_exp * tm_e
    n_tiles = n_rows // tm_e
    h32, meta, cnt = _router(xf, norm_g[3], mods, 3, seq, odd_router_w[0], odd_router_b[0])
    counts = cnt[0, :n_exp].astype(jnp.int32)
    tiles_e = (counts + tm_e - 1) // tm_e
    ends = jnp.cumsum(tiles_e)
    start_rows = (ends - tiles_e) * tm_e
    e_idx = meta[:, 0:2].astype(jnp.int32)
    pos = (jnp.take(start_rows, e_idx) + meta[:, 2:4].astype(jnp.int32)).reshape(2 * t)
    tid = jnp.arange(n_tiles, dtype=jnp.int32)
    group_last = jnp.any((tid[:, None] == ends[None, :] - 1) & (tiles_e[None, :] > 0), axis=1)
    zero_flag = (group_last | (tid >= ends[-1])).astype(jnp.int32)
    xs = _dispatch(h32, pos, zero_flag, n_rows, tm_e, seq)
    ys = _expert_ffn(xs, ends, tiles_e, counts, odd_exp_w1[0], odd_exp_w3[0], odd_exp_w2[0], tm_e)
    out = _combine_final(ys, pos, xf, meta, mods, 3, seq, final_norm_g)
    return out.reshape(batch, seq, d)
```

```python
import functools

import jax
import jax.numpy as jnp
from jax import lax
from jax.experimental import pallas as pl
from jax.experimental.pallas import tpu as pltpu

F32 = jnp.float32
BF16 = jnp.bfloat16

EPS = 1e-6
ROPE_THETA = 10000.0
LRU_C = 8.0
MLA_NOPE = 128
MLA_ROPE = 64
MLA_V = 128
QK_HEAD = MLA_NOPE + 2 * MLA_ROPE
LANE = 128
ADA_ROWS = 16
VMEM_LIMIT_BYTES = 56 * 1024 * 1024


def _params(grid_rank):
    return pltpu.CompilerParams(dimension_semantics=("arbitrary",) * grid_rank,
                                vmem_limit_bytes=VMEM_LIMIT_BYTES)


def _tile(dim, pref, mult=LANE):
    if dim <= pref:
        return dim
    t = (pref // mult) * mult
    while t > mult and dim % t:
        t -= mult
    assert dim % t == 0, (dim, pref)
    return t


def _round_up(x, m):
    return (x + m - 1) // m * m


CAST_ROWS = 256


def _cast_rows(src_ref, dst_ref):
    rows = src_ref.shape[0]
    step = CAST_ROWS if rows % CAST_ROWS == 0 else rows
    for r in range(0, rows, step):
        dst_ref[r:r + step, :] = src_ref[r:r + step, :].astype(dst_ref.dtype)


def _ada_kernel(c_ref, w_ref, b_ref, o_ref):
    c = c_ref[...]
    sc = (c * jax.nn.sigmoid(c)).astype(BF16)
    o_ref[...] = jnp.dot(sc, w_ref[...].astype(BF16), preferred_element_type=F32) + b_ref[...]


def _ada_modulation(c, ada_w, ada_b):
    nmod, d, d3 = ada_w.shape
    b = c.shape[0]
    tn = _tile(d3, 512)
    cp = jnp.zeros((ADA_ROWS, d), F32).at[:b].set(c)
    out = pl.pallas_call(
        _ada_kernel,
        out_shape=jax.ShapeDtypeStruct((nmod, ADA_ROWS, d3), F32),
        grid=(nmod, d3 // tn),
        in_specs=[
            pl.BlockSpec((ADA_ROWS, d), lambda i, j: (0, 0)),
            pl.BlockSpec((None, d, tn), lambda i, j: (i, 0, j)),
            pl.BlockSpec((None, 1, tn), lambda i, j: (i, 0, j)),
        ],
        out_specs=pl.BlockSpec((None, ADA_ROWS, tn), lambda i, j: (i, 0, j)),
        compiler_params=_params(2),
        name="ada_modulation",
    )(cp, ada_w, ada_b.reshape(nmod, 1, d3))
    return out[:, :b].reshape(nmod, b, 3, d)


def _mod_spec(li, d_blk, rows_per_batch_tiles, col_map=None):
    if col_map is None:
        return pl.BlockSpec((None, None, 3, d_blk),
                            lambda m, *_: (li, m // rows_per_batch_tiles, 0, 0))
    return pl.BlockSpec((None, None, 3, d_blk),
                        lambda m, n, *_: (li, m // rows_per_batch_tiles, 0, n))


NORM_CHUNK = 16


def _norm_rows(x, gain, shift):
    var = jnp.mean(x * x, axis=-1, keepdims=True)
    return (x * lax.rsqrt(var + EPS)) * gain + shift


def _norm_kernel(x_ref, g_ref, mod_ref, h_ref, *, tm):
    gain = g_ref[...] * (1.0 + mod_ref[1:2, :])
    shift = mod_ref[0:1, :]

    def body(r, carry):
        r0 = pl.multiple_of(r * NORM_CHUNK, NORM_CHUNK)
        x = x_ref[pl.ds(r0, NORM_CHUNK), :]
        h_ref[pl.ds(r0, NORM_CHUNK), :] = _norm_rows(x, gain, shift).astype(h_ref.dtype)
        return carry
    lax.fori_loop(0, tm // NORM_CHUNK, body, 0, unroll=2)


def _norm_mod(x, g, mods, li, seq):
    t, d = x.shape
    tm = _tile(seq, 512, NORM_CHUNK)
    return pl.pallas_call(
        functools.partial(_norm_kernel, tm=tm),
        out_shape=jax.ShapeDtypeStruct((t, d), BF16),
        grid=(t // tm,),
        in_specs=[
            pl.BlockSpec((tm, d), lambda m: (m, 0)),
            pl.BlockSpec((1, d), lambda m: (0, 0)),
            _mod_spec(li, d, seq // tm),
        ],
        out_specs=pl.BlockSpec((tm, d), lambda m: (m, 0)),
        compiler_params=_params(1),
        name=f"norm_mod_{li}",
    )(x, g.reshape(1, d), mods)


def _mm_kernel(*refs, n_lhs, n_w, n_extra, n_out, nk, epilogue, w_transposed=False):
    lhs = refs[:n_lhs]
    ws = refs[n_lhs:n_lhs + n_w]
    extra = refs[n_lhs + n_w:n_lhs + n_w + n_extra]
    outs = refs[n_lhs + n_w + n_extra:n_lhs + n_w + n_extra + n_out]
    accs = refs[n_lhs + n_w + n_extra + n_out:]

    def partial_product(w_ref):
        if w_transposed:
            return lax.dot_general(lhs[0][...], w_ref[...], (((1,), (1,)), ((), ())),
                                   preferred_element_type=F32)
        off, tot = 0, None
        for l_ref in lhs:
            kk = l_ref.shape[1]
            part = jnp.dot(l_ref[...], w_ref[off:off + kk, :], preferred_element_type=F32)
            tot = part if tot is None else tot + part
            off += kk
        return tot

    if nk == 1:
        _run_epilogue(epilogue, [partial_product(w) for w in ws], extra, outs)
        return

    k = pl.program_id(2)

    @pl.when(k == 0)
    def _():
        for acc, w in zip(accs, ws):
            acc[...] = partial_product(w)

    @pl.when(k > 0)
    def _():
        for acc, w in zip(accs, ws):
            acc[...] += partial_product(w)

    @pl.when(k == nk - 1)
    def _():
        _run_epilogue(epilogue, [acc[...] for acc in accs], extra, outs)


def _matmul(lhs_list, w_list, w_col_offsets, epilogue, extras, extra_specs, out_shapes,
            *, tm, tn, tk=None, w_transposed=False, name):
    t = lhs_list[0].shape[0]
    ktot = sum(l.shape[1] for l in lhs_list)
    n = out_shapes[0].shape[1]
    if w_transposed:
        assert tk is None and len(lhs_list) == 1
    if tk is None:
        nk = 1
        lhs_specs = [pl.BlockSpec((tm, l.shape[1]), lambda m, j, k: (m, 0)) for l in lhs_list]
        w_rows = ktot
    else:
        assert len(lhs_list) == 1 and ktot % tk == 0
        nk = ktot // tk
        lhs_specs = [pl.BlockSpec((tm, tk), lambda m, j, k: (m, k))]
        w_rows = tk
    if w_transposed:
        w_specs = [pl.BlockSpec((tn, ktot), functools.partial(lambda m, j, k, off: (j + off, 0), off=off))
                   for off in w_col_offsets]
    else:
        w_specs = [pl.BlockSpec((w_rows, tn), functools.partial(lambda m, j, k, off: (k, j + off), off=off))
                   for off in w_col_offsets]
    kern = functools.partial(_mm_kernel, n_lhs=len(lhs_list), n_w=len(w_list), n_extra=len(extras),
                             n_out=len(out_shapes), nk=nk, epilogue=epilogue, w_transposed=w_transposed)
    scratch = [pltpu.VMEM((tm, tn), F32) for _ in w_list] if nk > 1 else []
    return pl.pallas_call(
        kern,
        out_shape=out_shapes,
        grid=(t // tm, n // tn, nk),
        in_specs=lhs_specs + w_specs + list(extra_specs),
        out_specs=[pl.BlockSpec((tm, tn), lambda m, j, k: (m, j)) for _ in out_shapes],
        scratch_shapes=scratch,
        compiler_params=_params(3),
        name=name,
    )(*lhs_list, *w_list, *extras)


def _ws_kernel(*refs, n_lhs, n_w, n_extra, n_out, tn, nj, col_offsets, rem, epilogue):
    lhs = refs[:n_lhs]
    w_hbm = refs[n_lhs:n_lhs + n_w]
    extra = refs[n_lhs + n_w:n_lhs + n_w + n_extra]
    outs = refs[n_lhs + n_w + n_extra:n_lhs + n_w + n_extra + n_out]
    stage, wb, sem = refs[n_lhs + n_w + n_extra + n_out:]
    j = pl.program_id(0)
    m = pl.program_id(1)

    def wcopy(col_tile, k, width):
        cols = pl.ds(pl.multiple_of(col_tile * tn + col_offsets[k], LANE), width)
        dst = stage.at[k] if width == tn else stage.at[k, :, 0:width]
        return pltpu.make_async_copy(w_hbm[k].at[0, :, cols], dst, sem.at[k])

    def for_tile(col_tile, fn):
        if rem == tn:
            for k in range(n_w):
                fn(wcopy(col_tile, k, tn))
            return

        @pl.when(col_tile < nj - 1)
        def _():
            for k in range(n_w):
                fn(wcopy(col_tile, k, tn))

        @pl.when(col_tile == nj - 1)
        def _():
            for k in range(n_w):
                fn(wcopy(col_tile, k, rem))

    @pl.when((j == 0) & (m == 0))
    def _():
        for_tile(j, lambda c: c.start())

    @pl.when(m == 0)
    def _():
        for_tile(j, lambda c: c.wait())
        for k in range(n_w):
            _cast_rows(stage.at[k], wb.at[k])
        if rem != tn:
            @pl.when(j == nj - 1)
            def _():
                for k in range(n_w):
                    wb[k, :, rem:] = jnp.zeros((wb.shape[1], tn - rem), BF16)

        @pl.when(j + 1 < nj)
        def _():
            for_tile(j + 1, lambda c: c.start())

    accs = []
    for k in range(n_w):
        off, tot = 0, None
        for l_ref in lhs:
            kk = l_ref.shape[1]
            part = jnp.dot(l_ref[...], wb[k, off:off + kk, :], preferred_element_type=F32)
            tot = part if tot is None else tot + part
            off += kk
        accs.append(tot)
    _run_epilogue(epilogue, accs, extra, outs)


def _ws_matmul(lhs_list, w_list, col_offsets, n_valid, epilogue, extras, extra_specs, out_shapes,
               *, tm, tn, name):
    t = lhs_list[0].shape[0]
    ktot = sum(l.shape[1] for l in lhs_list)
    n_out = out_shapes[0].shape[1]
    nj = n_out // tn
    rem = n_valid - (nj - 1) * tn
    assert 0 < rem <= tn and (rem == tn or nj > 1)
    kern = functools.partial(_ws_kernel, n_lhs=len(lhs_list), n_w=len(w_list), n_extra=len(extras),
                             n_out=len(out_shapes), tn=tn, nj=nj, col_offsets=tuple(col_offsets), rem=rem,
                             epilogue=epilogue)
    return pl.pallas_call(
        kern,
        out_shape=out_shapes,
        grid=(nj, t // tm),
        in_specs=[pl.BlockSpec((tm, l.shape[1]), lambda j, m: (m, 0)) for l in lhs_list]
                 + [pl.BlockSpec(memory_space=pl.ANY) for _ in w_list] + list(extra_specs),
        out_specs=[pl.BlockSpec((tm, tn), lambda j, m: (m, j)) for _ in out_shapes],
        scratch_shapes=[pltpu.VMEM((len(w_list), ktot, tn), F32), pltpu.VMEM((len(w_list), ktot, tn), BF16),
                        pltpu.SemaphoreType.DMA((len(w_list),))],
        compiler_params=_params(2),
        name=name,
    )(*lhs_list, *w_list, *extras)


def _ws_residual_matmul(lhs_list, w, x, mods, li, seq, *, tm, tn, name):
    t, d = x.shape
    tiles_per_batch = seq // tm
    extras = [x, mods]
    extra_specs = [pl.BlockSpec((tm, tn), lambda j, m: (m, j)),
                   pl.BlockSpec((None, None, 3, tn), lambda j, m: (li, m // tiles_per_batch, 0, j))]
    return _ws_matmul(lhs_list, [w], [0], d, _epi_residual, extras, extra_specs,
                      [jax.ShapeDtypeStruct((t, d), F32)], tm=tm, tn=tn, name=name)[0]


EPILOGUE_ROWS = 128


def _run_epilogue(epilogue, accs, extra, outs):
    rows = accs[0].shape[0]
    step = EPILOGUE_ROWS if rows % EPILOGUE_ROWS == 0 else rows
    for r in range(0, rows, step):
        epilogue([a[r:r + step] for a in accs], extra, outs, slice(r, r + step))


def _epi_store(accs, extra, outs, rs):
    outs[0][rs, :] = accs[0].astype(outs[0].dtype)


def _epi_residual(accs, extra, outs, rs):
    x_ref, mod_ref = extra
    outs[0][rs, :] = x_ref[rs, :] + mod_ref[2:3, :] * accs[0]


def _epi_swiglu(accs, extra, outs, rs):
    a, b = accs
    outs[0][rs, :] = (jax.nn.silu(a) * b).astype(outs[0].dtype)


def _epi_gelu_pair(accs, extra, outs, rs):
    a, b = accs
    outs[0][rs, :] = jax.nn.gelu(a, approximate=True).astype(outs[0].dtype)
    outs[1][rs, :] = b.astype(outs[1].dtype)


def _residual_matmul(lhs_list, w, x, mods, li, seq, *, tm, tn, tk=None, name):
    t, d = x.shape
    extras = [x, mods]
    extra_specs = [pl.BlockSpec((tm, tn), lambda m, j, k: (m, j)),
                   _mod_spec(li, tn, seq // tm, col_map=True)]
    return _matmul(lhs_list, [w], [0], _epi_residual, extras, extra_specs,
                   [jax.ShapeDtypeStruct((t, d), F32)], tm=tm, tn=tn, tk=tk, name=name)[0]


def _rope_kernel(pos_ref, inv_ref, o_ref):
    ang = pos_ref[...].astype(F32) * inv_ref[...]
    lane = lax.broadcasted_iota(jnp.int32, ang.shape, 1)
    o_ref[...] = jnp.where(lane < MLA_ROPE, jnp.cos(ang), jnp.sin(ang))


def _rope_table(positions):
    t = positions.size
    tm = _tile(t, 1024, 8)
    half = MLA_ROPE // 2
    inv = 1.0 / (ROPE_THETA ** (jnp.arange(0, MLA_ROPE, 2, dtype=F32) / MLA_ROPE))
    inv4 = jnp.tile(inv, 4).reshape(1, 4 * half)
    return pl.pallas_call(
        _rope_kernel,
        out_shape=jax.ShapeDtypeStruct((t, LANE), F32),
        grid=(t // tm,),
        in_specs=[pl.BlockSpec((tm, 1), lambda m: (m, 0)),
                  pl.BlockSpec((1, LANE), lambda m: (0, 0))],
        out_specs=pl.BlockSpec((tm, LANE), lambda m: (m, 0)),
        compiler_params=_params(1),
        name="rope_table",
    )(positions.reshape(t, 1), inv4)


def _qproj_kernel(cq_ref, g_ref, rt_ref, w_ref, o_ref, nq_ref, *, heads_per_tile, scale):
    @pl.when(pl.program_id(1) == 0)
    def _():
        cq = cq_ref[...].astype(F32)
        var = jnp.mean(cq * cq, axis=-1, keepdims=True)
        nq_ref[...] = ((cq * lax.rsqrt(var + EPS)) * g_ref[...]).astype(BF16)

    res = jnp.dot(nq_ref[...], w_ref[...], preferred_element_type=F32)
    rt = rt_ref[...] * scale
    for hh in range(heads_per_tile):
        c0 = hh * QK_HEAD
        o_ref[:, c0:c0 + MLA_NOPE] = (res[:, c0:c0 + MLA_NOPE] * scale).astype(o_ref.dtype)
        o_ref[:, c0 + MLA_NOPE:c0 + QK_HEAD] = (res[:, c0 + MLA_NOPE:c0 + QK_HEAD] * rt).astype(o_ref.dtype)


def _q_projection(proj, g_q, rt, w_uq_p, heads, q_lora):
    t = proj.shape[0]
    tm = _tile(t, 1024, 16)
    hpt = min(heads, 4)
    tn = hpt * QK_HEAD
    scale = (MLA_NOPE + MLA_ROPE) ** -0.5
    return pl.pallas_call(
        functools.partial(_qproj_kernel, heads_per_tile=hpt, scale=scale),
        out_shape=jax.ShapeDtypeStruct((t, heads * QK_HEAD), BF16),
        grid=(t // tm, heads // hpt),
        in_specs=[
            pl.BlockSpec((tm, q_lora), lambda m, j: (m, 0)),
            pl.BlockSpec((1, q_lora), lambda m, j: (0, 0)),
            pl.BlockSpec((tm, LANE), lambda m, j: (m, 0)),
            pl.BlockSpec((q_lora, tn), lambda m, j: (0, j)),
        ],
        out_specs=pl.BlockSpec((tm, tn), lambda m, j: (m, j)),
        scratch_shapes=[pltpu.VMEM((tm, q_lora), BF16)],
        compiler_params=_params(2),
        name="q_projection",
    )(proj, g_q.reshape(1, q_lora), rt, w_uq_p)


def _kvproj_kernel(ckv_ref, kr_ref, g_ref, rt_ref, w_ref, kn_ref, v_ref, kro_ref, *, hv):
    ckv = ckv_ref[...].astype(F32)
    var = jnp.mean(ckv * ckv, axis=-1, keepdims=True)
    nkv = ((ckv * lax.rsqrt(var + EPS)) * g_ref[...]).astype(BF16)
    res = jnp.dot(nkv, w_ref[...], preferred_element_type=F32)
    kn_ref[...] = res[:, :hv].astype(kn_ref.dtype)
    v_ref[...] = res[:, hv:].astype(v_ref.dtype)
    kv = kr_ref[...].astype(F32) * rt_ref[...]
    kro_ref[...] = (kv + pltpu.roll(kv, MLA_ROPE, 1)).astype(kro_ref.dtype)


def _kv_projection(proj, g_kv, rt, w_ukv_p, heads, q_lora, kv_lora):
    t = proj.shape[0]
    tm = _tile(t, 512, 16)
    hv = heads * MLA_NOPE
    return pl.pallas_call(
        functools.partial(_kvproj_kernel, hv=hv),
        out_shape=[jax.ShapeDtypeStruct((t, hv), BF16), jax.ShapeDtypeStruct((t, hv), BF16),
                   jax.ShapeDtypeStruct((t, LANE), BF16)],
        grid=(t // tm,),
        in_specs=[
            pl.BlockSpec((tm, kv_lora), lambda m: (m, q_lora // kv_lora)),
            pl.BlockSpec((tm, LANE), lambda m: (m, (q_lora + kv_lora) // LANE)),
            pl.BlockSpec((1, kv_lora), lambda m: (0, 0)),
            pl.BlockSpec((tm, LANE), lambda m: (m, 0)),
            pl.BlockSpec((kv_lora, 2 * hv), lambda m: (0, 0)),
        ],
        out_specs=[pl.BlockSpec((tm, hv), lambda m: (m, 0)), pl.BlockSpec((tm, hv), lambda m: (m, 0)),
                   pl.BlockSpec((tm, LANE), lambda m: (m, 0))],
        compiler_params=_params(1),
        name="kv_projection",
    )(proj, proj, g_kv.reshape(1, kv_lora), rt, w_ukv_p)


def _attn_kernel(q_ref, kn_ref, kr_ref, v_ref, o_ref, *, tq, hpb):
    qi = pl.program_id(2)
    ones = jnp.ones((tq, MLA_V), BF16)

    def block(ki, carry, masked):
        ks = pl.multiple_of(ki * tq, tq)
        kr = kr_ref[pl.ds(ks, tq), :]
        new = []
        for hh in range(hpb):
            m, acc = carry[hh]
            q = q_ref[:, hh * QK_HEAD:(hh + 1) * QK_HEAD]
            k = jnp.concatenate([kn_ref[pl.ds(ks, tq), hh * MLA_NOPE:(hh + 1) * MLA_NOPE], kr], axis=1)
            s = lax.dot_general(q, k, (((1,), (1,)), ((), ())), preferred_element_type=F32)
            if masked:
                row = lax.broadcasted_iota(jnp.int32, s.shape, 0)
                col = lax.broadcasted_iota(jnp.int32, s.shape, 1)
                s = jnp.where(col <= row, s, -jnp.inf)
            m_new = jnp.maximum(m, jnp.max(s, axis=1, keepdims=True))
            alpha = jnp.exp(m - m_new)
            p = jnp.exp(s - m_new).astype(BF16)
            v1 = jnp.concatenate([v_ref[pl.ds(ks, tq), hh * MLA_V:(hh + 1) * MLA_V], ones], axis=1)
            acc = alpha * acc + jnp.dot(p, v1, preferred_element_type=F32)
            new.append((m_new, acc))
        return tuple(new)

    init = tuple((jnp.full((tq, 1), -jnp.inf, F32), jnp.zeros((tq, 2 * MLA_V), F32)) for _ in range(hpb))
    carry = lax.fori_loop(0, qi, lambda ki, c: block(ki, c, False), init)
    carry = block(qi, carry, True)
    for hh in range(hpb):
        acc = carry[hh][1]
        o_ref[:, hh * MLA_V:(hh + 1) * MLA_V] = (acc[:, :MLA_V] / acc[:, MLA_V:]).astype(o_ref.dtype)


def _attention(q, kn, kr, v, batch, seq, heads):
    t = q.shape[0]
    tq = _tile(seq, 512, 16)
    nq = seq // tq
    hpb = 2 if heads % 2 == 0 else 1
    return pl.pallas_call(
        functools.partial(_attn_kernel, tq=tq, hpb=hpb),
        out_shape=jax.ShapeDtypeStruct((t, heads * MLA_V), BF16),
        grid=(batch, heads // hpb, nq),
        in_specs=[
            pl.BlockSpec((tq, hpb * QK_HEAD), lambda b, h, i: (b * nq + i, h)),
            pl.BlockSpec((seq, hpb * MLA_NOPE), lambda b, h, i: (b, h)),
            pl.BlockSpec((seq, LANE), lambda b, h, i: (b, 0)),
            pl.BlockSpec((seq, hpb * MLA_V), lambda b, h, i: (b, h)),
        ],
        out_specs=pl.BlockSpec((tq, hpb * MLA_V), lambda b, h, i: (b * nq + i, h)),
        compiler_params=_params(3),
        name="mla_attention",
    )(q, kn, kr, v)


HALO = 16


def _conv3_kernel(gb_ref, gc_ref, u_ref, gch_ref, uh_ref, w_ref, o_ref, ext_ref, *, tm, tiles_per_seq, taps):
    first = (pl.program_id(0) % tiles_per_seq) == 0
    halo = gch_ref[...].astype(F32) * uh_ref[...].astype(F32)
    ext_ref[0:HALO, :] = jnp.where(first, 0.0, halo)
    ext_ref[HALO:, :] = gc_ref[...].astype(F32) * u_ref[...].astype(F32)
    y = None
    for j in range(taps):
        off = HALO - (taps - 1 - j)
        term = w_ref[j:j + 1, :] * ext_ref[off:off + tm, :]
        y = term if y is None else y + term
    o_ref[...] = (gb_ref[...].astype(F32) * y).astype(o_ref.dtype)


def _gated_conv(proj, conv_w, seq, conv_dim):
    t = proj.shape[0]
    taps = conv_w.shape[0]
    tm = _tile(seq, 512, HALO)
    tc = _tile(conv_dim, 512)
    ob, oc, ou = 0, conv_dim // tc, 2 * conv_dim // tc
    hb = tm // HALO

    def halo_map(off):
        return lambda m, c: (jnp.maximum(m * hb - 1, 0), off + c)

    return pl.pallas_call(
        functools.partial(_conv3_kernel, tm=tm, tiles_per_seq=seq // tm, taps=taps),
        out_shape=jax.ShapeDtypeStruct((t, conv_dim), BF16),
        grid=(t // tm, conv_dim // tc),
        in_specs=[
            pl.BlockSpec((tm, tc), lambda m, c: (m, ob + c)),
            pl.BlockSpec((tm, tc), lambda m, c: (m, oc + c)),
            pl.BlockSpec((tm, tc), lambda m, c: (m, ou + c)),
            pl.BlockSpec((HALO, tc), halo_map(oc)),
            pl.BlockSpec((HALO, tc), halo_map(ou)),
            pl.BlockSpec((taps, tc), lambda m, c: (0, c)),
        ],
        out_specs=pl.BlockSpec((tm, tc), lambda m, c: (m, c)),
        scratch_shapes=[pltpu.VMEM((tm + HALO, tc), F32)],
        compiler_params=_params(2),
        name="gated_conv3",
    )(proj, proj, proj, proj, proj, conv_w)


GROUP = 8


def _lru_kernel(xb_ref, gbr_ref, cw_ref, cb_ref, wa_ref, ba_ref, wx_ref, bx_ref, lam_ref, o_ref,
                ext_ref, a_ref, b_ref, hc_ref, *, tm, hd, hp, taps):
    @pl.when(pl.program_id(2) == 0)
    def _():
        ext_ref[0:GROUP, :] = jnp.zeros((GROUP, ext_ref.shape[1]), F32)
        hc_ref[...] = jnp.zeros(hc_ref.shape, F32)

    ext_ref[GROUP:, :] = xb_ref[...].astype(F32)
    ext = ext_ref[...]
    xb = cb_ref[...]
    for j in range(taps):
        back = taps - 1 - j
        shifted = ext if back == 0 else pltpu.roll(ext, back, 0)
        xb = xb + cw_ref[j:j + 1, :] * shifted[GROUP:GROUP + tm, :]
    ext_ref[0:GROUP, :] = ext_ref[tm:tm + GROUP, :]

    xb16 = xb.astype(BF16)
    rs, is_ = [], []
    for h in range(hp):
        xh = xb16[:, h * hd:(h + 1) * hd]
        rs.append(jnp.dot(xh, wa_ref[h], preferred_element_type=F32))
        is_.append(jnp.dot(xh, wx_ref[h], preferred_element_type=F32))
    r = jax.nn.sigmoid(jnp.concatenate(rs, axis=1) + ba_ref[...])
    ig = jax.nn.sigmoid(jnp.concatenate(is_, axis=1) + bx_ref[...])
    lam = lam_ref[...]
    log_sig = -(jnp.maximum(-lam, 0.0) + jnp.log(1.0 + jnp.exp(-jnp.abs(lam))))
    log_a = (LRU_C * r) * log_sig
    a = jnp.exp(log_a)
    th = jnp.tanh(log_a)
    mult = jnp.sqrt(-2.0 * th / (1.0 - th))
    a_ref[...] = a
    b_ref[...] = mult * (ig * xb)

    rowg = lax.broadcasted_iota(jnp.int32, (GROUP, a_ref.shape[1]), 0)

    def group(gi, hc):
        r0 = pl.multiple_of(gi * GROUP, GROUP)
        ag = a_ref[pl.ds(r0, GROUP), :]
        bg = b_ref[pl.ds(r0, GROUP), :]
        for dist in (1, 2, 4):
            keep = rowg >= dist
            ap = jnp.where(keep, pltpu.roll(ag, dist, 0), 1.0)
            bp = jnp.where(keep, pltpu.roll(bg, dist, 0), 0.0)
            bg = ag * bp + bg
            ag = ag * ap
        hs = ag * hc + bg
        y = hs * gbr_ref[pl.ds(r0, GROUP), :].astype(F32)
        b_ref[pl.ds(r0, GROUP), :] = y
        return jnp.broadcast_to(hs[GROUP - 1:GROUP, :], hs.shape)

    hc_ref[...] = lax.fori_loop(0, tm // GROUP, group, hc_ref[...], unroll=4)
    o_ref[...] = b_ref[...].astype(o_ref.dtype)


def _rglru(xb_pre, gate_br, conv_w, conv_b, wa, ba, wx, bx, lam, batch, seq):
    t, width = xb_pre.shape
    heads, hd, _ = wa.shape
    taps = conv_w.shape[0]
    hp = 4 if heads % 4 == 0 else (2 if heads % 2 == 0 else 1)
    c = hp * hd
    tm = _tile(seq, 512, 16)
    nt = seq // tm
    row = lambda b, h, i: (b * nt + i, h)
    vec = lambda b, h, i: (0, h)
    return pl.pallas_call(
        functools.partial(_lru_kernel, tm=tm, hd=hd, hp=hp, taps=taps),
        out_shape=jax.ShapeDtypeStruct((t, width), BF16),
        grid=(batch, heads // hp, nt),
        in_specs=[
            pl.BlockSpec((tm, c), row),
            pl.BlockSpec((tm, c), row),
            pl.BlockSpec((taps, c), vec),
            pl.BlockSpec((1, c), vec),
            pl.BlockSpec((hp, hd, hd), lambda b, h, i: (h, 0, 0)),
            pl.BlockSpec((1, c), vec),
            pl.BlockSpec((hp, hd, hd), lambda b, h, i: (h, 0, 0)),
            pl.BlockSpec((1, c), vec),
            pl.BlockSpec((1, c), vec),
        ],
        out_specs=pl.BlockSpec((tm, c), row),
        scratch_shapes=[pltpu.VMEM((tm + GROUP, c), F32), pltpu.VMEM((tm, c), F32),
                        pltpu.VMEM((tm, c), F32), pltpu.VMEM((GROUP, c), F32)],
        compiler_params=_params(3),
        name="rglru",
    )(xb_pre, gate_br, conv_w, conv_b.reshape(1, width), wa, ba.reshape(1, width), wx,
      bx.reshape(1, width), lam.reshape(1, width))


def _router_kernel(x_ref, g_ref, mod_ref, rw_ref, rb_ref, h_ref, meta_ref, cnt_ref, run_ref, hi_ref, lo_ref,
                   *, tm, n_exp):
    @pl.when(pl.program_id(0) == 0)
    def _():
        run_ref[...] = jnp.zeros(run_ref.shape, F32)

    gain = g_ref[...] * (1.0 + mod_ref[1:2, :])
    shift = mod_ref[0:1, :]

    def body(r, carry):
        r0 = pl.multiple_of(r * NORM_CHUNK, NORM_CHUNK)
        h = _norm_rows(x_ref[pl.ds(r0, NORM_CHUNK), :], gain, shift)
        hi = h.astype(BF16)
        h_ref[pl.ds(r0, NORM_CHUNK), :] = h
        hi_ref[pl.ds(r0, NORM_CHUNK), :] = hi
        lo_ref[pl.ds(r0, NORM_CHUNK), :] = (h - hi.astype(F32)).astype(BF16)
        return carry
    lax.fori_loop(0, tm // NORM_CHUNK, body, 0, unroll=2)

    logits = (jnp.dot(hi_ref[...], rw_ref[0], preferred_element_type=F32)
              + jnp.dot(lo_ref[...], rw_ref[0], preferred_element_type=F32)
              + jnp.dot(hi_ref[...], rw_ref[1], preferred_element_type=F32)) + rb_ref[...]
    lane = lax.broadcasted_iota(jnp.int32, logits.shape, 1)
    lg = jnp.where(lane < n_exp, logits, -jnp.inf)
    m1 = jnp.max(lg, axis=1, keepdims=True)
    i1 = jnp.min(jnp.where(lg == m1, lane, LANE), axis=1, keepdims=True)
    lg2 = jnp.where(lane == i1, -jnp.inf, lg)
    m2 = jnp.max(lg2, axis=1, keepdims=True)
    i2 = jnp.min(jnp.where(lg2 == m2, lane, LANE), axis=1, keepdims=True)
    e2 = jnp.exp(m2 - m1)
    den = 1.0 + e2
    w1 = 1.0 / den
    w2 = e2 / den

    hit1 = lane == i1
    hit2 = lane == i2
    sel = jnp.where(hit1 | hit2, 1.0, 0.0)
    rowi = lax.broadcasted_iota(jnp.int32, (tm, tm), 0)
    coli = lax.broadcasted_iota(jnp.int32, (tm, tm), 1)
    earlier = jnp.where(coli < rowi, 1.0, 0.0).astype(BF16)
    rank = jnp.dot(earlier, sel.astype(BF16), preferred_element_type=F32) + run_ref[0:1, :]
    r1 = jnp.sum(jnp.where(hit1, rank, 0.0), axis=1, keepdims=True)
    r2 = jnp.sum(jnp.where(hit2, rank, 0.0), axis=1, keepdims=True)
    run_ref[...] = run_ref[...] + jnp.sum(sel, axis=0, keepdims=True)
    cnt_ref[...] = run_ref[...]

    meta = jnp.where(lane == 0, i1.astype(F32), 0.0)
    meta = jnp.where(lane == 1, i2.astype(F32), meta)
    meta = jnp.where(lane == 2, r1, meta)
    meta = jnp.where(lane == 3, r2, meta)
    meta = jnp.where(lane == 4, w1, meta)
    meta = jnp.where(lane == 5, w2, meta)
    meta_ref[...] = meta


def _router(x, g, mods, li, seq, router_w, router_b):
    t, d = x.shape
    n_exp = router_w.shape[1]
    tm = _tile(seq, 256, 16)
    rw = jnp.zeros((d, LANE), F32).at[:, :n_exp].set(router_w)
    rw_hi = rw.astype(BF16)
    rw = jnp.stack([rw_hi, (rw - rw_hi.astype(F32)).astype(BF16)])
    rb = jnp.zeros((1, LANE), F32).at[0, :n_exp].set(router_b)
    return pl.pallas_call(
        functools.partial(_router_kernel, tm=tm, n_exp=n_exp),
        out_shape=[jax.ShapeDtypeStruct((t, d), F32), jax.ShapeDtypeStruct((t, LANE), F32),
                   jax.ShapeDtypeStruct((GROUP, LANE), F32)],
        grid=(t // tm,),
        in_specs=[
            pl.BlockSpec((tm, d), lambda m: (m, 0)),
            pl.BlockSpec((1, d), lambda m: (0, 0)),
            _mod_spec(li, d, seq // tm),
            pl.BlockSpec((2, d, LANE), lambda m: (0, 0, 0)),
            pl.BlockSpec((1, LANE), lambda m: (0, 0)),
        ],
        out_specs=[pl.BlockSpec((tm, d), lambda m: (m, 0)), pl.BlockSpec((tm, LANE), lambda m: (m, 0)),
                   pl.BlockSpec((GROUP, LANE), lambda m: (0, 0))],
        scratch_shapes=[pltpu.VMEM((GROUP, LANE), F32), pltpu.VMEM((tm, d), BF16), pltpu.VMEM((tm, d), BF16)],
        compiler_params=_params(1),
        name="moe_router",
    )(x, g.reshape(1, d), mods, rw, rb)


DMA_LOOP_UNROLL = 8


def _dispatch_kernel(pos_ref, zf_ref, h_ref, xs_hbm, zbuf, zsem, sem, *, tm, n_tiles, zr):
    @pl.when(pl.program_id(0) == 0)
    def _():
        zbuf[...] = jnp.zeros(zbuf.shape, zbuf.dtype)

        def zero_copy(c):
            return pltpu.make_async_copy(zbuf, xs_hbm.at[pl.ds(c * zr, zr), :], zsem)

        def zstart(c, carry):
            @pl.when(zf_ref[c] > 0)
            def _():
                zero_copy(c).start()
            return carry
        lax.fori_loop(0, n_tiles, zstart, 0)

        def zwait(c, carry):
            @pl.when(zf_ref[c] > 0)
            def _():
                zero_copy(c).wait()
            return carry
        lax.fori_loop(0, n_tiles, zwait, 0)

    base = pl.program_id(0) * tm

    def row_copy(r, slot):
        return pltpu.make_async_copy(h_ref.at[pl.ds(r, 1), :], xs_hbm.at[pl.ds(slot, 1), :], sem)

    def issue(r, carry):
        row_copy(r, pos_ref[2 * (base + r)]).start()
        row_copy(r, pos_ref[2 * (base + r) + 1]).start(priority=1)
        return carry
    lax.fori_loop(0, tm, issue, 0, unroll=DMA_LOOP_UNROLL)

    def wait(r, carry):
        row_copy(r, 0).wait()
        row_copy(r, 0).wait()
        return carry
    lax.fori_loop(0, tm, wait, 0, unroll=DMA_LOOP_UNROLL)


def _dispatch(h, pos, zero_flag, n_rows, zr, seq):
    t, d = h.shape
    tm = _tile(seq, 256, 8)
    n_tiles = n_rows // zr
    return pl.pallas_call(
        functools.partial(_dispatch_kernel, tm=tm, n_tiles=n_tiles, zr=zr),
        out_shape=jax.ShapeDtypeStruct((n_rows, d), h.dtype),
        grid_spec=pltpu.PrefetchScalarGridSpec(
            num_scalar_prefetch=2,
            grid=(t // tm,),
            in_specs=[pl.BlockSpec((tm, d), lambda m, pos_ref, zf_ref: (m, 0))],
            out_specs=pl.BlockSpec(memory_space=pl.ANY),
            scratch_shapes=[pltpu.VMEM((zr, d), h.dtype), pltpu.SemaphoreType.DMA(()),
                            pltpu.SemaphoreType.DMA(())],
        ),
        compiler_params=_params(1),
        name="moe_dispatch",
    )(pos, zero_flag, h)


ITEM_ZERO, ITEM_COMPUTE, ITEM_FIRST, ITEM_FIRST_MORE = 0, 1, 2, 3


def _work_items(ends, tiles_e, counts, tm_e, n_tiles, ncol):
    n_exp = ends.shape[0]
    s = jnp.arange(ncol * n_tiles, dtype=jnp.int32)
    n_valid = ncol * ends[-1]
    e = jnp.minimum(jnp.sum((s[:, None] >= ncol * ends[None, :]).astype(jnp.int32), axis=1), n_exp - 1)
    te = jnp.maximum(jnp.take(tiles_e, e), 1)
    local = s - ncol * jnp.take(ends - tiles_e, e)
    col = local // te
    tile = jnp.take(ends - tiles_e, e) + local % te
    valid = s < n_valid
    first = valid & (local % te == 0)
    nxt = jnp.minimum(s + te, ncol * n_tiles - 1)
    more = (s + te) < n_valid
    rem = s - n_valid
    tile = jnp.where(valid, tile, ends[-1] + rem // ncol)
    col = jnp.where(valid, col, rem % ncol)
    kind = jnp.where(valid, jnp.where(first, jnp.where(more, ITEM_FIRST_MORE, ITEM_FIRST), ITEM_COMPUTE),
                     ITEM_ZERO)
    rows = jnp.take(counts, e) - (tile - jnp.take(ends - tiles_e, e)) * tm_e
    half = (valid & (rows <= tm_e // 2)).astype(jnp.int32)
    return tile, col, e, kind.astype(jnp.int32), jnp.take(e, nxt), jnp.take(col, nxt), half


def _stage_weights(w_hbm_list, stage, wb, sem, kind, s, e_ref, c_ref, nxe_ref, nxc_ref, tn):
    def wcopy(e, c, k):
        cols = pl.ds(pl.multiple_of(c * tn, tn), tn)
        return pltpu.make_async_copy(w_hbm_list[k].at[e, :, cols], stage.at[k], sem.at[k])

    @pl.when(s == 0)
    def _():
        for k in range(len(w_hbm_list)):
            wcopy(e_ref[0], c_ref[0], k).start()

    @pl.when(kind >= ITEM_FIRST)
    def _():
        for k in range(len(w_hbm_list)):
            wcopy(e_ref[s], c_ref[s], k).wait()
            _cast_rows(stage.at[k], wb.at[k])

        @pl.when(kind == ITEM_FIRST_MORE)
        def _():
            for k in range(len(w_hbm_list)):
                wcopy(nxe_ref[s], nxc_ref[s], k).start()


def _row_tile_cases(kind, half, o_ref, compute):
    tm = o_ref.shape[0]

    @pl.when((kind >= ITEM_COMPUTE) & (half == 0))
    def _():
        compute(slice(0, tm))

    @pl.when((kind >= ITEM_COMPUTE) & (half > 0))
    def _():
        compute(slice(0, tm // 2))
        o_ref[tm // 2:, :] = jnp.zeros((tm - tm // 2, o_ref.shape[1]), o_ref.dtype)

    @pl.when(kind == ITEM_ZERO)
    def _():
        o_ref[...] = jnp.zeros(o_ref.shape, o_ref.dtype)


def _gm1_kernel(tile_ref, c_ref, e_ref, kind_ref, nxe_ref, nxc_ref, half_ref, x_ref, w1_hbm, w3_hbm, o_ref,
                stage, wb, sem, *, tf):
    s = pl.program_id(0)
    kind = kind_ref[s]
    _stage_weights((w1_hbm, w3_hbm), stage, wb, sem, kind, s, e_ref, c_ref, nxe_ref, nxc_ref, tf)

    def compute(rs):
        x = x_ref[rs, :].astype(BF16)
        a = jnp.dot(x, wb[0], preferred_element_type=F32)
        b = jnp.dot(x, wb[1], preferred_element_type=F32)
        o_ref[rs, :] = (jax.nn.silu(a) * b).astype(o_ref.dtype)

    _row_tile_cases(kind, half_ref[s], o_ref, compute)


def _gm2_kernel(tile_ref, c_ref, e_ref, kind_ref, nxe_ref, nxc_ref, half_ref, a_ref, w_hbm, o_ref,
                stage, wb, sem, *, tn):
    s = pl.program_id(0)
    kind = kind_ref[s]
    _stage_weights((w_hbm,), stage, wb, sem, kind, s, e_ref, c_ref, nxe_ref, nxc_ref, tn)

    def compute(rs):
        o_ref[rs, :] = jnp.dot(a_ref[rs, :], wb[0], preferred_element_type=F32).astype(o_ref.dtype)

    _row_tile_cases(kind, half_ref[s], o_ref, compute)


def _expert_ffn(xs, ends, tiles_e, counts, w1, w3, w2, tm_e):
    r, d = xs.shape
    _, _, f = w1.shape
    n_tiles = r // tm_e
    tf = _tile(f, 512)
    tn = _tile(d, 1024)
    row_map = lambda s, tile, col, *_: (tile[s], 0)
    out_map = lambda s, tile, col, *_: (tile[s], col[s])

    a_s = pl.pallas_call(
        functools.partial(_gm1_kernel, tf=tf),
        out_shape=jax.ShapeDtypeStruct((r, f), BF16),
        grid_spec=pltpu.PrefetchScalarGridSpec(
            num_scalar_prefetch=7,
            grid=(n_tiles * (f // tf),),
            in_specs=[pl.BlockSpec((tm_e, d), row_map),
                      pl.BlockSpec(memory_space=pl.ANY), pl.BlockSpec(memory_space=pl.ANY)],
            out_specs=pl.BlockSpec((tm_e, tf), out_map),
            scratch_shapes=[pltpu.VMEM((2, d, tf), F32), pltpu.VMEM((2, d, tf), BF16),
                            pltpu.SemaphoreType.DMA((2,))],
        ),
        compiler_params=_params(1),
        name="moe_expert_up",
    )(*_work_items(ends, tiles_e, counts, tm_e, n_tiles, f // tf), xs, w1, w3)

    return pl.pallas_call(
        functools.partial(_gm2_kernel, tn=tn),
        out_shape=jax.ShapeDtypeStruct((r, d), F32),
        grid_spec=pltpu.PrefetchScalarGridSpec(
            num_scalar_prefetch=7,
            grid=(n_tiles * (d // tn),),
            in_specs=[pl.BlockSpec((tm_e, f), row_map),
                      pl.BlockSpec(memory_space=pl.ANY)],
            out_specs=pl.BlockSpec((tm_e, tn), out_map),
            scratch_shapes=[pltpu.VMEM((1, f, tn), F32), pltpu.VMEM((1, f, tn), BF16),
                            pltpu.SemaphoreType.DMA((1,))],
        ),
        compiler_params=_params(1),
        name="moe_expert_down",
    )(*_work_items(ends, tiles_e, counts, tm_e, n_tiles, d // tn), a_s, w2)


def _combine_kernel(pos_ref, ys_hbm, x_ref, meta_ref, mod_ref, g_ref, o_ref, buf, sem, *, tm, n_steps):
    i = pl.program_id(0)

    def row_copy(half, r, k, src_row):
        return pltpu.make_async_copy(ys_hbm.at[pl.ds(src_row, 1), :], buf.at[half, k, pl.ds(r, 1), :],
                                     sem.at[half, k])

    def issue(step):
        half = step % 2
        base = step * tm

        def body(r, carry):
            for k in range(2):
                row_copy(half, r, k, pos_ref[2 * (base + r) + k]).start(priority=k)
            return carry
        lax.fori_loop(0, tm, body, 0, unroll=DMA_LOOP_UNROLL)

    @pl.when(i == 0)
    def _():
        issue(i)

    @pl.when(i + 1 < n_steps)
    def _():
        issue(i + 1)

    half = i % 2

    def wait(r, carry):
        for k in range(2):
            row_copy(half, r, k, 0).wait()
        return carry
    lax.fori_loop(0, tm, wait, 0, unroll=DMA_LOOP_UNROLL)

    y = meta_ref[:, 4:5] * buf[half, 0] + meta_ref[:, 5:6] * buf[half, 1]
    xn = x_ref[...] + mod_ref[2:3, :] * y
    var = jnp.mean(xn * xn, axis=-1, keepdims=True)
    o_ref[...] = (xn * lax.rsqrt(var + EPS)) * g_ref[...]


def _combine_final(ys, pos, x, meta, mods, li, seq, g_final):
    t, d = x.shape
    tm = _tile(seq, 128, 8)
    return pl.pallas_call(
        functools.partial(_combine_kernel, tm=tm, n_steps=t // tm),
        out_shape=jax.ShapeDtypeStruct((t, d), F32),
        grid_spec=pltpu.PrefetchScalarGridSpec(
            num_scalar_prefetch=1,
            grid=(t // tm,),
            in_specs=[pl.BlockSpec(memory_space=pl.ANY),
                      pl.BlockSpec((tm, d), lambda m, pos_ref: (m, 0)),
                      pl.BlockSpec((tm, LANE), lambda m, pos_ref: (m, 0)),
                      pl.BlockSpec((None, None, 3, d), lambda m, pos_ref: (li, m // (seq // tm), 0, 0)),
                      pl.BlockSpec((1, d), lambda m, pos_ref: (0, 0))],
            out_specs=pl.BlockSpec((tm, d), lambda m, pos_ref: (m, 0)),
            scratch_shapes=[pltpu.VMEM((2, 2, tm, d), F32), pltpu.SemaphoreType.DMA((2, 2))],
        ),
        compiler_params=_params(1),
        name="moe_combine_final_norm",
    )(pos, ys, x, meta, mods, g_final.reshape(1, d))


def _rot_half_cols(w):
    half = w.shape[-1] // 2
    return jnp.concatenate([-w[..., half:], w[..., :half]], axis=-1)


def _prep_w_in(w_in, q_lora, kv_lora, n_lat):
    w_t = jnp.swapaxes(w_in, 0, 1)
    d = w_t.shape[1]
    o1 = q_lora + kv_lora
    o2 = o1 + MLA_ROPE
    half = MLA_ROPE // 2
    k_rope = w_t[o1:o2]
    rot = jnp.concatenate([-k_rope[half:], k_rope[:half]], axis=0)
    pad = jnp.zeros((n_lat - o2 - MLA_ROPE, d), w_t.dtype)
    lat = jnp.concatenate([w_t[:o2], rot, pad], axis=0).astype(BF16)
    return lat, w_t[o2:].astype(BF16)


def _prep_w_uq(w_uq, heads):
    ql = w_uq.shape[0]
    w = w_uq.reshape(ql, heads, MLA_NOPE + MLA_ROPE)
    rope = w[..., MLA_NOPE:]
    return jnp.concatenate([w[..., :MLA_NOPE], rope, _rot_half_cols(rope)], axis=-1
                           ).reshape(ql, heads * QK_HEAD).astype(BF16)


def _prep_w_ukv(w_ukv, heads):
    kvl = w_ukv.shape[0]
    w = w_ukv.reshape(kvl, heads, MLA_NOPE + MLA_V)
    return jnp.concatenate([w[..., :MLA_NOPE].reshape(kvl, heads * MLA_NOPE),
                            w[..., MLA_NOPE:].reshape(kvl, heads * MLA_V)], axis=1).astype(BF16)


def kernel(x, c, positions, ada_w, ada_b, norm_g, even_w_in, even_q_norm_g, even_kv_norm_g, even_w_uq, even_w_ukv, even_conv_w, even_w_out, even_ffn_w1, even_ffn_w3, even_ffn_w2, odd_w_in, odd_conv_w, odd_conv_b, odd_gate_a_w, odd_gate_a_b, odd_gate_x_w, odd_gate_x_b, odd_lambda, odd_w_out, odd_router_w, odd_router_b, odd_exp_w1, odd_exp_w3, odd_exp_w2, final_norm_g):
    batch, seq, d = x.shape
    t = batch * seq
    q_lora = even_q_norm_g.shape[1]
    kv_lora = even_kv_norm_g.shape[1]
    heads = even_w_uq.shape[2] // (MLA_NOPE + MLA_ROPE)
    conv_dim = even_conv_w.shape[2]
    d_ff = even_ffn_w1.shape[2]
    n_exp = odd_router_w.shape[2]
    assert even_w_in.shape[0] == 1 and odd_w_in.shape[0] == 1, "one layer of each type"

    xf = x.reshape(t, d)
    mods = _ada_modulation(c, ada_w, ada_b)
    rt = _rope_table(positions)
    tm = _tile(seq, 1024, 16)
    tm_dual = _tile(seq, 512, 16)

    n_lat = _round_up(q_lora + kv_lora + LANE, 256)
    w_lat, w_cv = _prep_w_in(even_w_in[0], q_lora, kv_lora, n_lat)
    h = _norm_mod(xf, norm_g[0], mods, 0, seq)
    lat = _matmul([h], [w_lat], [0], _epi_store, [], [], [jax.ShapeDtypeStruct((t, n_lat), BF16)],
                  tm=tm, tn=_tile(n_lat, 1024), w_transposed=True, name="in_proj_0_latents")[0]
    cv = _matmul([h], [w_cv], [0], _epi_store, [], [], [jax.ShapeDtypeStruct((t, 3 * conv_dim), BF16)],
                 tm=tm, tn=_tile(3 * conv_dim, 1024), w_transposed=True, name="in_proj_0_conv")[0]
    q = _q_projection(lat, even_q_norm_g[0], rt, _prep_w_uq(even_w_uq[0], heads), heads, q_lora)
    kn, v, kr = _kv_projection(lat, even_kv_norm_g[0], rt, _prep_w_ukv(even_w_ukv[0], heads),
                               heads, q_lora, kv_lora)
    attn = _attention(q, kn, kr, v, batch, seq, heads)
    conv = _gated_conv(cv, even_conv_w[0], seq, conv_dim)
    xf = _ws_residual_matmul([attn, conv], even_w_out, xf, mods, 0, seq, tm=tm, tn=_tile(d, 512),
                             name="out_proj_0")

    f_pad = _round_up(d_ff, 512)
    w2 = jnp.concatenate([even_ffn_w2[0].astype(BF16), jnp.zeros((f_pad - d_ff, d), BF16)], axis=0)
    h = _norm_mod(xf, norm_g[1], mods, 1, seq)
    act = _ws_matmul([h], [even_ffn_w1, even_ffn_w3], [0, 0], d_ff, _epi_swiglu, [], [],
                     [jax.ShapeDtypeStruct((t, f_pad), BF16)], tm=tm_dual, tn=_tile(f_pad, 512),
                     name="ffn_up")[0]
    xf = _residual_matmul([act], w2, xf, mods, 1, seq, tm=tm, tn=_tile(d, 1024),
                          tk=_tile(f_pad, 3072), name="ffn_down")

    width = odd_conv_w.shape[2]
    h = _norm_mod(xf, norm_g[2], mods, 2, seq)
    gate_br, xb_pre = _ws_matmul([h], [odd_w_in, odd_w_in], [0, width], width, _epi_gelu_pair, [], [],
                                 [jax.ShapeDtypeStruct((t, width), BF16)] * 2, tm=tm_dual,
                                 tn=_tile(width, 512), name="in_proj_1")
    y = _rglru(xb_pre, gate_br, odd_conv_w[0], odd_conv_b[0], odd_gate_a_w[0].astype(BF16),
               odd_gate_a_b[0], odd_gate_x_w[0].astype(BF16), odd_gate_x_b[0], odd_lambda[0], batch, seq)
    xf = _ws_residual_matmul([y], odd_w_out, xf, mods, 2, seq, tm=tm, tn=_tile(d, 512), name="out_proj_1")

    tm_e = _tile(seq, 512, 16)
    n_rows = 2 * t + n_exp * tm_e
    n_tiles = n_rows // tm_e
    h32, meta, cnt = _router(xf, norm_g[3], mods, 3, seq, odd_router_w[0], odd_router_b[0])
    counts = cnt[0, :n_exp].astype(jnp.int32)
    tiles_e = (counts + tm_e - 1) // tm_e
    ends = jnp.cumsum(tiles_e)
    start_rows = (ends - tiles_e) * tm_e
    e_idx = meta[:, 0:2].astype(jnp.int32)
    pos = (jnp.take(start_rows, e_idx) + meta[:, 2:4].astype(jnp.int32)).reshape(2 * t)
    tid = jnp.arange(n_tiles, dtype=jnp.int32)
    group_last = jnp.any((tid[:, None] == ends[None, :] - 1) & (tiles_e[None, :] > 0), axis=1)
    zero_flag = (group_last | (tid >= ends[-1])).astype(jnp.int32)
    xs = _dispatch(h32, pos, zero_flag, n_rows, tm_e, seq)
    ys = _expert_ffn(xs, ends, tiles_e, counts, odd_exp_w1[0], odd_exp_w3[0], odd_exp_w2[0], tm_e)
    out = _combine_final(ys, pos, xf, meta, mods, 3, seq, final_norm_g)
    return out.reshape(batch, seq, d)
```
